```python
import math
import jax, jax.numpy as jnp
from jax import lax
import numpy as np

D_MODEL = 1024
BATCH = 16
SEQ = 2048
DEPTH = 2

GRID_W = 64
CTX_LEN = 256
EPS = 1e-6
ROPE_BASE = 10000.0
Q_BLOCK = 128
MLA_HEADS = 8
MLA_NOPE = 64
MLA_ROPE = 32
MLA_V = 64
Q_LORA = 256
KV_LORA = 128
MLA_WIDTH = MLA_HEADS * MLA_V
LRU_HEADS = 4
LRU_WIDTH = 256
LRU_BLOCK = LRU_WIDTH // LRU_HEADS
CONV_W = 4
LRU_C = 8.0
SGU_GROUPS = 4
SGU_WIDTH = 256
SGU_GROUP_DIM = SGU_WIDTH // SGU_GROUPS
CHUNK = 128
MIX_WIDTH = MLA_WIDTH + LRU_WIDTH + SGU_WIDTH
IN_SIZES = (Q_LORA, KV_LORA, MLA_ROPE, LRU_WIDTH, LRU_WIDTH, SGU_WIDTH, SGU_WIDTH)
IN_WIDTH = sum(IN_SIZES)
IN_SPLITS = tuple(int(s) for s in np.cumsum(IN_SIZES)[:-1])
D_FF_RAW = -(-8 * D_MODEL // 3)
D_FF = (D_FF_RAW + 255) // 256 * 256

kernel_name = 'hybrid_mla_rglru_sgu_dit_block'

F32 = jnp.float32


def rms_norm(x, g):
    xf = x.astype(F32)
    y = xf * lax.rsqrt(jnp.mean(xf * xf, axis=-1, keepdims=True) + EPS)
    return (y * g.astype(F32)).astype(x.dtype)


def modulate(h, shift, scale):
    return h * (1 + scale[:, None]) + shift[:, None]


def rope_2d_tables(seq_len):
    rows = seq_len // GRID_W
    row = jnp.repeat(jnp.arange(rows, dtype=F32), GRID_W)
    col = jnp.tile(jnp.arange(GRID_W, dtype=F32), rows)
    half = MLA_ROPE // 2
    freq = ROPE_BASE ** (-jnp.arange(0, half, 2, dtype=F32) / half)
    ar = row[:, None] * freq
    ac = col[:, None] * freq
    ang = jnp.concatenate([ar, ar, ac, ac], axis=-1)
    return jnp.cos(ang)[:, None, :], jnp.sin(ang)[:, None, :]


def _rot_half(v):
    v1, v2 = jnp.split(v, 2, axis=-1)
    return jnp.concatenate([-v2, v1], axis=-1)


def apply_rope_2d(x, cos, sin):
    xr, xc = jnp.split(x, 2, axis=-1)
    rotated = jnp.concatenate([_rot_half(xr), _rot_half(xc)], axis=-1)
    return x * cos.astype(x.dtype) + rotated * sin.astype(x.dtype)


def block_attention(q, k, v):
    B, S, H, dq = q.shape
    nb = S // Q_BLOCK
    qb = jnp.moveaxis(q.reshape(B, nb, Q_BLOCK, H, dq), 1, 0)
    scale = 1.0 / math.sqrt(dq)

    def one_block(qi):
        s = jnp.einsum('bqhd,bkhd->bhqk', qi, k).astype(F32) * scale
        p = jax.nn.softmax(s, axis=-1).astype(v.dtype)
        return jnp.einsum('bhqk,bkhd->bqhd', p, v)

    o = lax.map(one_block, qb)
    return jnp.moveaxis(o, 0, 1).reshape(B, S, H * v.shape[-1])


def mla_q(qa, g, w):
    B, T, _ = qa.shape
    return (rms_norm(qa, g) @ w).reshape(B, T, MLA_HEADS, MLA_NOPE + MLA_ROPE)


def mla_kv(kva, k_rope, g, w):
    B, T, _ = kva.shape
    kv = (rms_norm(kva, g) @ w).reshape(B, T, MLA_HEADS, MLA_NOPE + MLA_V)
    k_nope, v = kv[..., :MLA_NOPE], kv[..., MLA_NOPE:]
    k = jnp.concatenate([k_nope, jnp.broadcast_to(k_rope, (B, T, MLA_HEADS, MLA_ROPE))], axis=-1)
    return k, v


def short_conv(x, w, b):
    C = x.shape[-1]
    y = lax.conv_general_dilated(
        x, w[:, None, :].astype(x.dtype), window_strides=(1,),
        padding=[(CONV_W // 2, CONV_W - 1 - CONV_W // 2)],
        dimension_numbers=('NWC', 'WIO', 'NWC'), feature_group_count=C)
    return y + b


def lru_gates(xc, w_r, b_r, w_i, b_i, lam):
    B, T, C = xc.shape
    xb = xc.reshape(B, T, LRU_HEADS, LRU_BLOCK)
    r = jax.nn.sigmoid(jnp.einsum('bthi,hij->bthj', xb, w_r).reshape(B, T, C) + b_r).astype(F32)
    i = jax.nn.sigmoid(jnp.einsum('bthi,hij->bthj', xb, w_i).reshape(B, T, C) + b_i)
    log_a = -LRU_C * r * jax.nn.softplus(-lam.astype(F32))
    a = jnp.exp(log_a)
    b = jnp.sqrt(-jnp.expm1(2.0 * log_a)) * (i * xc).astype(F32)
    return a, b


def _lin_combine(left, right):
    a_l, b_l = left
    a_r, b_r = right
    return a_l * a_r, a_r * b_l + b_r


def linear_scan(a, b, h0, reverse):
    if reverse:
        a = jnp.flip(a, axis=1)
        b = jnp.flip(b, axis=1)
    b = b.at[:, 0].add(a[:, 0] * h0)
    _, h = lax.associative_scan(_lin_combine, (a, b), axis=1)
    return jnp.flip(h, axis=1) if reverse else h


def sgu(u, v, g, w_s, b_s):
    B, T, _ = u.shape
    n = T // CHUNK
    u = jax.nn.gelu(u)
    vg = jax.nn.gelu(v).reshape(B, n, CHUNK, SGU_GROUPS, SGU_GROUP_DIM)
    vg = rms_norm(vg, g.reshape(SGU_GROUPS, SGU_GROUP_DIM))
    s = jnp.einsum('gpq,bnqgc->bnpgc', w_s, vg) + b_s.T[None, None, :, :, None]
    return u * s.reshape(B, T, SGU_WIDTH)


def merge_groups(att, rec, sp, g, w_out):
    y = jnp.concatenate([
        rms_norm(att, g[:MLA_WIDTH]),
        rms_norm(rec, g[MLA_WIDTH:MLA_WIDTH + LRU_WIDTH]),
        rms_norm(sp, g[MLA_WIDTH + LRU_WIDTH:])], axis=-1)
    return y @ w_out


def token_mixers(h_l, h_c, cos, sin, w_in, q_a_norm, w_q_b, kv_a_norm, w_kv_b, conv_w, conv_b,
                 lru_w_r, lru_b_r, lru_w_i, lru_b_i, lru_lam, sgu_norm, sgu_w, sgu_b,
                 out_norm, w_out, need_ctx):
    B = h_l.shape[0]
    qa_l, kva_l, kr_l, xr_l, gr_l, su_l, sv_l = jnp.split(h_l @ w_in, IN_SPLITS, axis=-1)
    qa_c, kva_c, kr_c, xr_c, gr_c, su_c, sv_c = jnp.split(h_c @ w_in, IN_SPLITS, axis=-1)

    k_l, v_l = mla_kv(kva_l, apply_rope_2d(kr_l[:, :, None, :], cos, sin), kv_a_norm, w_kv_b)
    k_c, v_c = mla_kv(kva_c, kr_c[:, :, None, :], kv_a_norm, w_kv_b)
    q_l = mla_q(qa_l, q_a_norm, w_q_b)
    q_l = jnp.concatenate([q_l[..., :MLA_NOPE], apply_rope_2d(q_l[..., MLA_NOPE:], cos, sin)], axis=-1)
    att_l = block_attention(q_l, jnp.concatenate([k_c, k_l], axis=1), jnp.concatenate([v_c, v_l], axis=1))

    xc_l = short_conv(xr_l, conv_w, conv_b)
    xc_c = short_conv(xr_c, conv_w, conv_b)
    zero = jnp.zeros((B, LRU_WIDTH), F32)
    hsum_l = jnp.zeros(xc_l.shape, F32)
    hsum_c = jnp.zeros(xc_c.shape, F32)
    for d, rev in ((0, False), (1, True)):
        a_c, b_c = lru_gates(xc_c, lru_w_r[d], lru_b_r[d], lru_w_i[d], lru_b_i[d], lru_lam[d])
        a_l, b_l = lru_gates(xc_l, lru_w_r[d], lru_b_r[d], lru_w_i[d], lru_b_i[d], lru_lam[d])
        hc = linear_scan(a_c, b_c, zero, rev)
        seed = hc[:, 0] if rev else hc[:, -1]
        hl = linear_scan(a_l, b_l, seed, rev)
        hsum_l = hsum_l + hl
        hsum_c = hsum_c + hc
    rec_l = hsum_l.astype(h_l.dtype) * jax.nn.gelu(gr_l)

    sp_l = sgu(su_l, sv_l, sgu_norm, sgu_w, sgu_b)

    y_l = merge_groups(att_l, rec_l, sp_l, out_norm, w_out)
    if not need_ctx:
        return y_l, None
    q_c = mla_q(qa_c, q_a_norm, w_q_b)
    att_c = block_attention(q_c, k_c, v_c)
    rec_c = hsum_c.astype(h_c.dtype) * jax.nn.gelu(gr_c)
    sp_c = sgu(su_c, sv_c, sgu_norm, sgu_w, sgu_b)
    y_c = merge_groups(att_c, rec_c, sp_c, out_norm, w_out)
    return y_l, y_c


def swiglu(h, w_in, w_out):
    g, u = jnp.split(h @ w_in, 2, axis=-1)
    return (jax.nn.silu(g) * u) @ w_out


def setup_inputs(seed: int = 0) -> dict:
    key = jax.random.key(seed)
    ks = jax.random.split(key, 28)

    def nrm(k, shape, fan_in, scale=1.0):
        return scale * fan_in ** -0.5 * jax.random.normal(k, shape, F32)

    def gain(k, shape):
        return 1.0 + 0.01 * jax.random.normal(k, shape, F32)

    a0 = jax.random.uniform(ks[19], (DEPTH, 2, LRU_WIDTH), F32, 0.9, 0.999)
    s0 = a0 ** (1.0 / LRU_C)
    lam = jnp.log(s0) - jnp.log1p(-s0)
    return {
        'x': jax.random.normal(ks[0], (BATCH, SEQ, D_MODEL), F32),
        'c': jax.random.normal(ks[1], (BATCH, D_MODEL), F32),
        'ctx': jax.random.normal(ks[2], (BATCH, CTX_LEN, D_MODEL), F32),
        'c_ctx': jax.random.normal(ks[3], (D_MODEL,), F32),
        'norm1': gain(ks[4], (DEPTH, D_MODEL)),
        'norm2': gain(ks[5], (DEPTH, D_MODEL)),
        'w_ada': nrm(ks[6], (DEPTH, D_MODEL, 6 * D_MODEL), D_MODEL, 0.3),
        'b_ada': 0.01 * jax.random.normal(ks[7], (DEPTH, 6 * D_MODEL), F32),
        'w_in': nrm(ks[8], (DEPTH, D_MODEL, IN_WIDTH), D_MODEL),
        'q_a_norm': gain(ks[9], (DEPTH, Q_LORA)),
        'w_q_b': nrm(ks[10], (DEPTH, Q_LORA, MLA_HEADS * (MLA_NOPE + MLA_ROPE)), Q_LORA),
        'kv_a_norm': gain(ks[11], (DEPTH, KV_LORA)),
        'w_kv_b': nrm(ks[12], (DEPTH, KV_LORA, MLA_HEADS * (MLA_NOPE + MLA_V)), KV_LORA),
        'conv_w': nrm(ks[13], (DEPTH, CONV_W, LRU_WIDTH), CONV_W),
        'conv_b': 0.01 * jax.random.normal(ks[14], (DEPTH, LRU_WIDTH), F32),
        'lru_w_r': nrm(ks[15], (DEPTH, 2, LRU_HEADS, LRU_BLOCK, LRU_BLOCK), LRU_BLOCK),
        'lru_b_r': 0.01 * jax.random.normal(ks[16], (DEPTH, 2, LRU_WIDTH), F32),
        'lru_w_i': nrm(ks[17], (DEPTH, 2, LRU_HEADS, LRU_BLOCK, LRU_BLOCK), LRU_BLOCK),
        'lru_b_i': 0.01 * jax.random.normal(ks[18], (DEPTH, 2, LRU_WIDTH), F32),
        'lru_lam': lam,
        'sgu_norm': gain(ks[20], (DEPTH, SGU_WIDTH)),
        'sgu_w': nrm(ks[21], (DEPTH, SGU_GROUPS, CHUNK, CHUNK), CHUNK),
        'sgu_b': 1.0 + 0.01 * jax.random.normal(ks[22], (DEPTH, SGU_GROUPS, CHUNK), F32),
        'out_norm': gain(ks[23], (DEPTH, MIX_WIDTH)),
        'w_out': nrm(ks[24], (DEPTH, MIX_WIDTH, D_MODEL), MIX_WIDTH),
        'w_ffn_in': nrm(ks[25], (DEPTH, D_MODEL, 2 * D_FF), D_MODEL),
        'w_ffn_out': nrm(ks[26], (DEPTH, D_FF, D_MODEL), D_FF),
        'final_norm': gain(ks[27], (D_MODEL,)),
    }


def reference(x, c, ctx, c_ctx, norm1, norm2, w_ada, b_ada, w_in, q_a_norm, w_q_b, kv_a_norm, w_kv_b,
              conv_w, conv_b, lru_w_r, lru_b_r, lru_w_i, lru_b_i, lru_lam, sgu_norm, sgu_w, sgu_b,
              out_norm, w_out, w_ffn_in, w_ffn_out, final_norm):
    cos, sin = rope_2d_tables(x.shape[1])
    h_ctx = ctx
    for l in range(DEPTH):
        last = l == DEPTH - 1
        mod_l = jax.nn.silu(c) @ w_ada[l] + b_ada[l]
        mod_c = (jax.nn.silu(c_ctx) @ w_ada[l] + b_ada[l])[None]
        sh1, sc1, g1, sh2, sc2, g2 = jnp.split(mod_l, 6, axis=-1)
        csh1, csc1, cg1, csh2, csc2, cg2 = jnp.split(mod_c, 6, axis=-1)

        hl = modulate(rms_norm(x, norm1[l]), sh1, sc1)
        hc = modulate(rms_norm(h_ctx, norm1[l]), csh1, csc1)
        y_l, y_c = token_mixers(hl, hc, cos, sin, w_in[l], q_a_norm[l], w_q_b[l], kv_a_norm[l], w_kv_b[l],
                                conv_w[l], conv_b[l], lru_w_r[l], lru_b_r[l], lru_w_i[l], lru_b_i[l],
                                lru_lam[l], sgu_norm[l], sgu_w[l], sgu_b[l], out_norm[l], w_out[l],
                                not last)
        x = x + g1[:, None] * y_l
        x = x + g2[:, None] * swiglu(modulate(rms_norm(x, norm2[l]), sh2, sc2), w_ffn_in[l], w_ffn_out[l])
        if not last:
            h_ctx = h_ctx + cg1[:, None] * y_c
            h_ctx = h_ctx + cg2[:, None] * swiglu(modulate(rms_norm(h_ctx, norm2[l]), csh2, csc2),
                                                 w_ffn_in[l], w_ffn_out[l])
    return rms_norm(x, final_norm)
```

```python
import functools
import math

import jax
import jax.numpy as jnp
import numpy as np
from jax import lax
from jax.experimental import pallas as pl
from jax.experimental.pallas import tpu as pltpu

F32 = jnp.float32
BF16 = jnp.bfloat16

D_MODEL = 1024
DEPTH = 2
GRID_W = 64
EPS = 1e-6
ROPE_BASE = 10000.0
HEADS = 8
NOPE = 64
ROPE = 32
V_DIM = 64
Q_LORA = 256
KV_LORA = 128
LRU_WIDTH = 256
LRU_HEADS = 4
CONV_W = 4
LRU_C = 8.0
SGU_GROUPS = 4
SGU_WIDTH = 256
SGU_GROUP_DIM = SGU_WIDTH // SGU_GROUPS
CHUNK = 128
D_FF = 2816
MLA_WIDTH = HEADS * V_DIM

LANE = 128
SUBLANE = 8
HEAD_PAD = LANE
VMEM_LIMIT = 56 * 1024 * 1024

IN_EXT = 1536
C_QA, C_KVA, C_KR, C_XR, C_GR, C_SU, C_SV = 0, 256, 384, 512, 768, 1024, 1280
FF_CHUNK = 1408


def _rms(x, g):
    ms = jnp.mean(x * x, axis=-1, keepdims=True)
    return x * lax.rsqrt(ms + EPS) * g


def _dot(a, b):
    return jnp.dot(a, b, preferred_element_type=F32)


def _const_spec(shape):
    zeros = (0,) * len(shape)
    return pl.BlockSpec(shape, lambda *_: zeros, pipeline_mode=pl.Buffered(1))


ADA_TILE = 512


def _ada_kernel(c_ref, w_ref, b_ref, o_ref):
    s = jax.nn.silu(c_ref[...]).astype(BF16)
    o_ref[0] = _dot(s, w_ref[0].astype(BF16)) + b_ref[0]


def _ada_call(cc, w_ada, b_ada):
    rows = cc.shape[0]
    n = w_ada.shape[-1]
    return pl.pallas_call(
        _ada_kernel,
        out_shape=jax.ShapeDtypeStruct((DEPTH, rows, n), F32),
        grid=(DEPTH, n // ADA_TILE),
        in_specs=[
            pl.BlockSpec((rows, D_MODEL), lambda l, j: (0, 0)),
            pl.BlockSpec((1, D_MODEL, ADA_TILE), lambda l, j: (l, 0, j)),
            pl.BlockSpec((1, 1, ADA_TILE), lambda l, j: (l, 0, j)),
        ],
        out_specs=pl.BlockSpec((1, rows, ADA_TILE), lambda l, j: (l, 0, j)),
        compiler_params=pltpu.CompilerParams(
            dimension_semantics=("arbitrary", "arbitrary"), vmem_limit_bytes=VMEM_LIMIT),
        name="ada_mod",
    )(cc, w_ada, b_ada.reshape(DEPTH, 1, n))


def _inproj_kernel(x_ref, mod_ref, n1_ref, win_ref, qan_ref, wq_ref, kvn_ref, wkv_ref,
                   cq_ref, sq_ref, tk_ref, wcat_ref, sbias_ref, sgn_ref, gmat_ref, gsp_ref,
                   q_ref, k_ref, v_ref, xr_ref, gr_ref, sp_ref, *, tm):
    x = x_ref[0]
    mod = mod_ref[0]
    h = _rms(x, n1_ref[...]) * (1.0 + mod[1:2]) + mod[0:1]
    z = _dot(h.astype(BF16), win_ref[...])

    qn = _rms(z[:, C_QA:C_QA + Q_LORA], qan_ref[...]).astype(BF16)
    qq = _dot(qn, wq_ref[...])
    cq = cq_ref[...]
    sq = sq_ref[...]
    rot0 = HEADS * HEAD_PAD
    for hd in range(HEADS):
        lo = hd * HEAD_PAD
        qh = qq[:, lo:lo + HEAD_PAD] * cq + qq[:, rot0 + lo:rot0 + lo + HEAD_PAD] * sq
        q_ref[0, hd] = qh.astype(BF16)

    kvn = _rms(z[:, C_KVA:C_KVA + KV_LORA], kvn_ref[...]).astype(BF16)
    t = z[:, C_KR:C_KR + LANE] * tk_ref[...]
    kro = t + pltpu.roll(t, LANE - ROPE, axis=1)
    lhs = jnp.concatenate([kvn, kro.astype(BF16)], axis=1)
    kv = _dot(lhs, wkv_ref[...])
    for hd in range(HEADS):
        k_ref[0, hd] = kv[:, hd * HEAD_PAD:(hd + 1) * HEAD_PAD].astype(BF16)
    v0 = HEADS * HEAD_PAD
    for pr in range(HEADS // 2):
        v_ref[0, pr] = kv[:, v0 + pr * LANE:v0 + (pr + 1) * LANE].astype(BF16)

    xr_ref[0] = z[:, C_XR:C_XR + LRU_WIDTH]
    gr_ref[0] = z[:, C_GR:C_GR + LRU_WIDTH]

    u = jax.nn.gelu(z[:, C_SU:C_SU + SGU_WIDTH])
    vg = jax.nn.gelu(z[:, C_SV:C_SV + SGU_WIDTH])
    v2 = vg * vg
    v2_hi = v2.astype(BF16)
    v2_lo = (v2 - v2_hi.astype(F32)).astype(BF16)
    gmat = gmat_ref[...]
    gms = _dot(v2_hi, gmat) + _dot(v2_lo, gmat)
    vb = (vg * lax.rsqrt(gms + EPS) * sgn_ref[...]).astype(BF16)
    grp = lax.broadcasted_iota(jnp.int32, (CHUNK, SGU_WIDTH), 1) // SGU_GROUP_DIM
    zero = jnp.zeros((CHUNK, SGU_WIDTH), BF16)
    wcat = wcat_ref[...]
    sbias = sbias_ref[...]
    parts = []
    for c in range(tm // CHUNK):
        vc = vb[c * CHUNK:(c + 1) * CHUNK]
        rhs = jnp.concatenate([jnp.where(grp == g, vc, zero) for g in range(SGU_GROUPS)], axis=0)
        s = _dot(wcat, rhs) + sbias
        parts.append(u[c * CHUNK:(c + 1) * CHUNK] * s)
    sp = jnp.concatenate(parts, axis=0) if len(parts) > 1 else parts[0]
    sp_ref[0] = _rms(sp, gsp_ref[...]).astype(BF16)


def _inproj_call(x, mod, per_batch_mod, tabs, lw, tm):
    B, T, _ = x.shape
    cq, sq, tk = tabs
    mod_map = (lambda b, t: (b, 0, 0)) if per_batch_mod else (lambda b, t: (0, 0, 0))
    tab_spec = pl.BlockSpec((tm, LANE), lambda b, t: (t, 0))
    in_specs = [
        pl.BlockSpec((1, tm, D_MODEL), lambda b, t: (b, t, 0)),
        pl.BlockSpec((1, 6, D_MODEL), mod_map),
        _const_spec((1, D_MODEL)),
        _const_spec((D_MODEL, IN_EXT)),
        _const_spec((1, Q_LORA)),
        _const_spec((Q_LORA, 2 * HEADS * HEAD_PAD)),
        _const_spec((1, KV_LORA)),
        _const_spec((2 * LANE, HEADS * HEAD_PAD + MLA_WIDTH)),
        tab_spec, tab_spec, tab_spec,
        _const_spec((CHUNK, SGU_GROUPS * CHUNK)),
        _const_spec((CHUNK, SGU_WIDTH)),
        _const_spec((1, SGU_WIDTH)),
        _const_spec((SGU_WIDTH, SGU_WIDTH)),
        _const_spec((1, SGU_WIDTH)),
    ]
    out_shape = [
        jax.ShapeDtypeStruct((B, HEADS, T, HEAD_PAD), BF16),
        jax.ShapeDtypeStruct((B, HEADS, T, HEAD_PAD), BF16),
        jax.ShapeDtypeStruct((B, HEADS // 2, T, LANE), BF16),
        jax.ShapeDtypeStruct((B, T, LRU_WIDTH), F32),
        jax.ShapeDtypeStruct((B, T, LRU_WIDTH), F32),
        jax.ShapeDtypeStruct((B, T, SGU_WIDTH), BF16),
    ]
    out_specs = [
        pl.BlockSpec((1, HEADS, tm, HEAD_PAD), lambda b, t: (b, 0, t, 0)),
        pl.BlockSpec((1, HEADS, tm, HEAD_PAD), lambda b, t: (b, 0, t, 0)),
        pl.BlockSpec((1, HEADS // 2, tm, LANE), lambda b, t: (b, 0, t, 0)),
        pl.BlockSpec((1, tm, LRU_WIDTH), lambda b, t: (b, t, 0)),
        pl.BlockSpec((1, tm, LRU_WIDTH), lambda b, t: (b, t, 0)),
        pl.BlockSpec((1, tm, SGU_WIDTH), lambda b, t: (b, t, 0)),
    ]
    return pl.pallas_call(
        functools.partial(_inproj_kernel, tm=tm),
        out_shape=out_shape,
        grid=(B, T // tm),
        in_specs=in_specs,
        out_specs=out_specs,
        compiler_params=pltpu.CompilerParams(
            dimension_semantics=("parallel", "parallel"), vmem_limit_bytes=VMEM_LIMIT),
        name="inproj",
    )(x, mod, lw["norm1"], lw["w_in"], lw["q_a_norm"], lw["w_q"], lw["kv_a_norm"], lw["w_kv"],
      cq, sq, tk, lw["sgu_wcat"], lw["sgu_bias"], lw["sgu_norm"], lw["gmat"], lw["g_sp"])


GATE_ROWS = 256
PAD_ROWS = SUBLANE


def _tile_scan(a, b, h_prev, row, reverse):
    for k in (1, 2, 4):
        if reverse:
            keep = row < SUBLANE - k
            sh = SUBLANE - k
        else:
            keep = row >= k
            sh = k
        a_s = jnp.where(keep, pltpu.roll(a, sh, axis=0), 1.0)
        b_s = jnp.where(keep, pltpu.roll(b, sh, axis=0), 0.0)
        b = a * b_s + b
        a = a * a_s
    h = a * h_prev + b
    last = h[0:1] if reverse else h[SUBLANE - 1:SUBLANE]
    return h, jnp.broadcast_to(last, h.shape)


def _lru_kernel(xrc_ref, xrl_ref, grc_ref, grl_ref, cw_ref, cb_ref, wg_ref, bg_ref, lam_ref,
                grec_ref, outc_ref, outl_ref, pad_ref, xc_ref, a_ref, b_ref, hs_ref, *, tc, tl):
    cw = cw_ref[...]
    cb = cb_ref[...]
    ttot = tc + tl

    def conv(src_ref, n, dst0):
        pad_ref[pl.ds(0, PAD_ROWS), :] = jnp.zeros((PAD_ROWS, LRU_WIDTH), F32)
        pad_ref[pl.ds(PAD_ROWS, n), :] = src_ref[0]
        pad_ref[pl.ds(PAD_ROWS + n, PAD_ROWS), :] = jnp.zeros((PAD_ROWS, LRU_WIDTH), F32)
        y = cb
        for j in range(CONV_W):
            y = y + cw[j:j + 1] * pad_ref[pl.ds(PAD_ROWS - CONV_W // 2 + j, n), :]
        xc_ref[pl.ds(dst0, n), :] = y

    conv(xrc_ref, tc, 0)
    conv(xrl_ref, tl, tc)

    nsp = -LRU_C * jax.nn.softplus(-lam_ref[...])
    wg = wg_ref[...]
    bg = bg_ref[...]
    tiles_per_chunk = GATE_ROWS // SUBLANE
    for c in range(ttot // GATE_ROWS):
        xcc = xc_ref[pl.ds(c * GATE_ROWS, GATE_ROWS), :]
        g = _dot(xcc.astype(BF16), wg) + bg
        for d in range(2):
            r = jax.nn.sigmoid(g[:, (2 * d) * LRU_WIDTH:(2 * d + 1) * LRU_WIDTH])
            i = jax.nn.sigmoid(g[:, (2 * d + 1) * LRU_WIDTH:(2 * d + 2) * LRU_WIDTH])
            log_a = nsp[d:d + 1] * r
            a = jnp.exp(log_a)
            bb = jnp.sqrt(-jnp.tanh(log_a) * (a * a + 1.0)) * (i * xcc)
            a_ref[d, pl.ds(c * tiles_per_chunk, tiles_per_chunk)] = a.reshape(
                tiles_per_chunk, SUBLANE, LRU_WIDTH)
            b_ref[d, pl.ds(c * tiles_per_chunk, tiles_per_chunk)] = bb.reshape(
                tiles_per_chunk, SUBLANE, LRU_WIDTH)

    row = lax.broadcasted_iota(jnp.int32, (SUBLANE, LRU_WIDTH), 0)
    nc = tc // SUBLANE
    nl = tl // SUBLANE

    def run(first_f, first_r, n, hf, hr):
        def body(i, carry):
            hf, hr = carry
            tf = first_f + i
            tr = first_r - i
            of, hf = _tile_scan(a_ref[0, tf], b_ref[0, tf], hf, row, False)
            orv, hr = _tile_scan(a_ref[1, tr], b_ref[1, tr], hr, row, True)
            hs_ref[0, tf] = of
            hs_ref[1, tr] = orv
            return hf, hr
        return lax.fori_loop(0, n, body, (hf, hr))

    zero = jnp.zeros((SUBLANE, LRU_WIDTH), F32)
    hf, hr = run(0, nc - 1, nc, zero, zero)
    run(nc, nc + nl - 1, nl, hf, hr)

    grec = grec_ref[...]
    for c in range(ttot // GATE_ROWS):
        sl = pl.ds(c * tiles_per_chunk, tiles_per_chunk)
        hsum = (hs_ref[0, sl] + hs_ref[1, sl]).reshape(GATE_ROWS, LRU_WIDTH)
        r0 = c * GATE_ROWS
        if r0 < tc:
            gate = grc_ref[0, pl.ds(r0, GATE_ROWS), :]
        else:
            gate = grl_ref[0, pl.ds(r0 - tc, GATE_ROWS), :]
        y = _rms(hsum * jax.nn.gelu(gate), grec).astype(BF16)
        if r0 < tc:
            outc_ref[0, pl.ds(r0, GATE_ROWS), :] = y
        else:
            outl_ref[0, pl.ds(r0 - tc, GATE_ROWS), :] = y


def _lru_call(xr_c, xr_l, gr_c, gr_l, lw):
    B, tc, _ = xr_c.shape
    tl = xr_l.shape[1]
    ttot = tc + tl
    assert tc % GATE_ROWS == 0 and tl % GATE_ROWS == 0
    seq = lambda n: pl.BlockSpec((1, n, LRU_WIDTH), lambda b: (b, 0, 0))
    return pl.pallas_call(
        functools.partial(_lru_kernel, tc=tc, tl=tl),
        out_shape=[jax.ShapeDtypeStruct((B, tc, LRU_WIDTH), BF16),
                   jax.ShapeDtypeStruct((B, tl, LRU_WIDTH), BF16)],
        grid=(B,),
        in_specs=[seq(tc), seq(tl), seq(tc), seq(tl),
                  _const_spec((CONV_W, LRU_WIDTH)), _const_spec((1, LRU_WIDTH)),
                  _const_spec((LRU_WIDTH, 4 * LRU_WIDTH)), _const_spec((1, 4 * LRU_WIDTH)),
                  _const_spec((2, LRU_WIDTH)), _const_spec((1, LRU_WIDTH))],
        out_specs=[seq(tc), seq(tl)],
        scratch_shapes=[
            pltpu.VMEM((tl + 2 * PAD_ROWS, LRU_WIDTH), F32),
            pltpu.VMEM((ttot, LRU_WIDTH), F32),
            pltpu.VMEM((2, ttot // SUBLANE, SUBLANE, LRU_WIDTH), F32),
            pltpu.VMEM((2, ttot // SUBLANE, SUBLANE, LRU_WIDTH), F32),
            pltpu.VMEM((2, ttot // SUBLANE, SUBLANE, LRU_WIDTH), F32),
        ],
        compiler_params=pltpu.CompilerParams(
            dimension_semantics=("parallel",), vmem_limit_bytes=VMEM_LIMIT),
        name="lru",
    )(xr_c, xr_l, gr_c, gr_l, lw["conv_w"], lw["conv_b"], lw["w_gates"], lw["b_gates"],
      lw["lru_lam"], lw["g_rec"])


def _attn_kernel(*refs, nseg):
    q_ref = refs[0]
    kv_refs = refs[1:1 + 2 * nseg]
    g_ref = refs[1 + 2 * nseg]
    o_ref = refs[2 + 2 * nseg]
    tq = q_ref.shape[2]
    lane = lax.broadcasted_iota(jnp.int32, (tq, LANE), 1)
    outs = []
    for pr in range(HEADS // 2):
        pair = []
        for sub in range(2):
            hd = 2 * pr + sub
            q = q_ref[0, hd]
            scores = [lax.dot_general(q, kv_refs[2 * s][0, hd], (((1,), (1,)), ((), ())),
                                      preferred_element_type=F32) for s in range(nseg)]
            m = scores[0].max(axis=-1, keepdims=True)
            for s in range(1, nseg):
                m = jnp.maximum(m, scores[s].max(axis=-1, keepdims=True))
            l = None
            o = None
            for s in range(nseg):
                p = jnp.exp2(scores[s] - m)
                ps = p.sum(axis=-1, keepdims=True)
                po = _dot(p.astype(BF16), kv_refs[2 * s + 1][0, pr])
                l = ps if l is None else l + ps
                o = po if o is None else o + po
            pair.append(o * (1.0 / l))
        outs.append(jnp.where(lane < V_DIM, pair[0], pair[1]))
    att = jnp.concatenate(outs, axis=1)
    o_ref[0] = _rms(att, g_ref[...]).astype(BF16)


def _attn_call(q, segs, g_att, tq):
    B, _, S, _ = q.shape
    nseg = len(segs)
    in_specs = [pl.BlockSpec((1, HEADS, tq, HEAD_PAD), lambda b, t: (b, 0, t, 0))]
    args = [q]
    for k, v in segs:
        tn = k.shape[2]
        in_specs.append(pl.BlockSpec((1, HEADS, tn, HEAD_PAD), lambda b, t: (b, 0, 0, 0)))
        in_specs.append(pl.BlockSpec((1, HEADS // 2, tn, LANE), lambda b, t: (b, 0, 0, 0)))
        args += [k, v]
    in_specs.append(_const_spec((1, MLA_WIDTH)))
    args.append(g_att)
    return pl.pallas_call(
        functools.partial(_attn_kernel, nseg=nseg),
        out_shape=jax.ShapeDtypeStruct((B, S, MLA_WIDTH), BF16),
        grid=(B, S // tq),
        in_specs=in_specs,
        out_specs=pl.BlockSpec((1, tq, MLA_WIDTH), lambda b, t: (b, t, 0)),
        compiler_params=pltpu.CompilerParams(
            dimension_semantics=("parallel", "arbitrary"), vmem_limit_bytes=VMEM_LIMIT),
        name="attn",
    )(*args)


def _out_kernel(x_ref, att_ref, rec_ref, sp_ref, mod_ref, n2_ref, wo_ref, wfi_ref, wfo_ref,
                fn_ref, o_ref, *, final):
    mod = mod_ref[0]
    y = _dot(att_ref[0], wo_ref[pl.ds(0, MLA_WIDTH), :])
    y = y + _dot(rec_ref[0], wo_ref[pl.ds(MLA_WIDTH, LRU_WIDTH), :])
    y = y + _dot(sp_ref[0], wo_ref[pl.ds(MLA_WIDTH + LRU_WIDTH, SGU_WIDTH), :])
    x1 = x_ref[0] + mod[2:3] * y
    h2 = (_rms(x1, n2_ref[...]) * (1.0 + mod[4:5]) + mod[3:4]).astype(BF16)
    acc = None
    for j in range(D_FF // FF_CHUNK):
        g = _dot(h2, wfi_ref[:, pl.ds(j * FF_CHUNK, FF_CHUNK)])
        u = _dot(h2, wfi_ref[:, pl.ds(D_FF + j * FF_CHUNK, FF_CHUNK)])
        act = (jax.nn.silu(g) * u).astype(BF16)
        part = _dot(act, wfo_ref[pl.ds(j * FF_CHUNK, FF_CHUNK), :])
        acc = part if acc is None else acc + part
    x2 = x1 + mod[5:6] * acc
    if final:
        x2 = _rms(x2, fn_ref[...])
    o_ref[0] = x2


def _out_call(x, att, rec, sp, mod, per_batch_mod, lw, final_norm, final, tm):
    B, T, _ = x.shape
    mod_map = (lambda b, t: (b, 0, 0)) if per_batch_mod else (lambda b, t: (0, 0, 0))
    tok = lambda w: pl.BlockSpec((1, tm, w), lambda b, t: (b, t, 0))
    return pl.pallas_call(
        functools.partial(_out_kernel, final=final),
        out_shape=jax.ShapeDtypeStruct((B, T, D_MODEL), F32),
        grid=(B, T // tm),
        in_specs=[tok(D_MODEL), tok(MLA_WIDTH), tok(LRU_WIDTH), tok(SGU_WIDTH),
                  pl.BlockSpec((1, 6, D_MODEL), mod_map),
                  _const_spec((1, D_MODEL)),
                  _const_spec((D_MODEL, D_MODEL)),
                  _const_spec((D_MODEL, 2 * D_FF)),
                  _const_spec((D_FF, D_MODEL)),
                  _const_spec((1, D_MODEL))],
        out_specs=tok(D_MODEL),
        compiler_params=pltpu.CompilerParams(
            dimension_semantics=("parallel", "parallel"), vmem_limit_bytes=VMEM_LIMIT),
        name="out_ffn",
    )(x, att, rec, sp, mod, lw["norm2"], lw["w_out"], lw["w_ffn_in"], lw["w_ffn_out"], final_norm)


def _rot_cols(w):
    q = ROPE // 4
    return jnp.concatenate([-w[..., q:2 * q], w[..., 0:q], -w[..., 3 * q:4 * q], w[..., 2 * q:3 * q]], axis=-1)


def _prep_layer(l, p):
    w_in = p["w_in"][l]
    splits = np.cumsum([Q_LORA, KV_LORA, ROPE, LRU_WIDTH, LRU_WIDTH, SGU_WIDTH])
    qa, kva, kr, xr, gr, su, sv = jnp.split(w_in, splits, axis=1)
    kr_blk = jnp.concatenate([kr, _rot_cols(kr), jnp.zeros((D_MODEL, LANE - 2 * ROPE), F32)], axis=1)
    w_in_ext = jnp.concatenate([qa, kva, kr_blk, xr, gr, su, sv], axis=1).astype(BF16)

    wq = p["w_q_b"][l].reshape(Q_LORA, HEADS, NOPE + ROPE)
    zpad = jnp.zeros((Q_LORA, HEADS, HEAD_PAD - NOPE - ROPE), F32)
    wq_plain = jnp.concatenate([wq, zpad], axis=-1).reshape(Q_LORA, HEADS * HEAD_PAD)
    wq_rot = jnp.concatenate([jnp.zeros((Q_LORA, HEADS, NOPE), F32), _rot_cols(wq[..., NOPE:]), zpad],
                             axis=-1).reshape(Q_LORA, HEADS * HEAD_PAD)
    w_q = jnp.concatenate([wq_plain, wq_rot], axis=1).astype(BF16)

    wkv = p["w_kv_b"][l].reshape(KV_LORA, HEADS, NOPE + V_DIM)
    wk = jnp.concatenate([wkv[..., :NOPE], jnp.zeros((KV_LORA, HEADS, HEAD_PAD - NOPE), F32)],
                         axis=-1).reshape(KV_LORA, HEADS * HEAD_PAD)
    wv = wkv[..., NOPE:].reshape(KV_LORA, MLA_WIDTH)
    place = jnp.zeros((LANE, HEADS, HEAD_PAD), F32)
    place = place.at[jnp.arange(ROPE), :, NOPE + jnp.arange(ROPE)].set(1.0)
    bottom = jnp.concatenate([place.reshape(LANE, HEADS * HEAD_PAD), jnp.zeros((LANE, MLA_WIDTH), F32)], axis=1)
    w_kv = jnp.concatenate([jnp.concatenate([wk, wv], axis=1), bottom], axis=0).astype(BF16)

    def block_diag(w):
        hh, bi, bj = w.shape
        eye = jnp.eye(hh, dtype=F32)
        return (eye[:, None, :, None] * w[:, :, None, :]).reshape(hh * bi, hh * bj)

    w_gates = jnp.concatenate([block_diag(p["lru_w_r"][l, 0]), block_diag(p["lru_w_i"][l, 0]),
                               block_diag(p["lru_w_r"][l, 1]), block_diag(p["lru_w_i"][l, 1])],
                              axis=1).astype(BF16)
    b_gates = jnp.concatenate([p["lru_b_r"][l, 0], p["lru_b_i"][l, 0],
                               p["lru_b_r"][l, 1], p["lru_b_i"][l, 1]])[None]

    sgu_wcat = jnp.transpose(p["sgu_w"][l], (1, 0, 2)).reshape(CHUNK, SGU_GROUPS * CHUNK).astype(BF16)
    sgu_bias = jnp.repeat(p["sgu_b"][l].T, SGU_GROUP_DIM, axis=1)
    gidx = jnp.arange(SGU_WIDTH) // SGU_GROUP_DIM
    gmat = ((gidx[:, None] == gidx[None, :]).astype(F32) / SGU_GROUP_DIM).astype(BF16)

    on = p["out_norm"][l]
    return {
        "norm1": p["norm1"][l][None], "norm2": p["norm2"][l][None],
        "w_in": w_in_ext, "q_a_norm": p["q_a_norm"][l][None], "w_q": w_q,
        "kv_a_norm": p["kv_a_norm"][l][None], "w_kv": w_kv,
        "conv_w": p["conv_w"][l], "conv_b": p["conv_b"][l][None],
        "w_gates": w_gates, "b_gates": b_gates, "lru_lam": p["lru_lam"][l],
        "sgu_wcat": sgu_wcat, "sgu_bias": sgu_bias, "sgu_norm": p["sgu_norm"][l][None], "gmat": gmat,
        "g_att": on[None, :MLA_WIDTH], "g_rec": on[None, MLA_WIDTH:MLA_WIDTH + LRU_WIDTH],
        "g_sp": on[None, MLA_WIDTH + LRU_WIDTH:],
        "w_out": p["w_out"][l].astype(BF16), "w_ffn_in": p["w_ffn_in"][l].astype(BF16),
        "w_ffn_out": p["w_ffn_out"][l].astype(BF16),
    }


def _rope_tables(seq_len, ctx_len):
    rows = seq_len // GRID_W
    row = jnp.repeat(jnp.arange(rows, dtype=F32), GRID_W)
    col = jnp.tile(jnp.arange(GRID_W, dtype=F32), rows)
    half = ROPE // 2
    freq = ROPE_BASE ** (-jnp.arange(0, half, 2, dtype=F32) / half)
    ar = row[:, None] * freq
    ac = col[:, None] * freq
    ang = jnp.concatenate([ar, ar, ac, ac], axis=-1)
    cos, sin = jnp.cos(ang), jnp.sin(ang)
    qs = math.log2(math.e) / math.sqrt(NOPE + ROPE)

    def q_tabs(c, s, n):
        cq = jnp.concatenate([jnp.ones((n, NOPE), F32), c, jnp.zeros((n, HEAD_PAD - NOPE - ROPE), F32)], axis=1)
        sq = jnp.concatenate([jnp.zeros((n, NOPE), F32), s, jnp.zeros((n, HEAD_PAD - NOPE - ROPE), F32)], axis=1)
        tk = jnp.concatenate([c, s, jnp.zeros((n, LANE - 2 * ROPE), F32)], axis=1)
        return cq * qs, sq * qs, tk

    lat = q_tabs(cos, sin, seq_len)
    ctx = q_tabs(jnp.ones((ctx_len, ROPE), F32), jnp.zeros((ctx_len, ROPE), F32), ctx_len)
    return lat, ctx


def kernel(x, c, ctx, c_ctx, norm1, norm2, w_ada, b_ada, w_in, q_a_norm, w_q_b, kv_a_norm, w_kv_b, conv_w, conv_b, lru_w_r, lru_b_r, lru_w_i, lru_b_i, lru_lam, sgu_norm, sgu_w, sgu_b, out_norm, w_out, w_ffn_in, w_ffn_out, final_norm):
    p = dict(norm1=norm1, norm2=norm2, w_in=w_in, q_a_norm=q_a_norm, w_q_b=w_q_b, kv_a_norm=kv_a_norm,
             w_kv_b=w_kv_b, conv_w=conv_w, conv_b=conv_b, lru_w_r=lru_w_r, lru_b_r=lru_b_r,
             lru_w_i=lru_w_i, lru_b_i=lru_b_i, lru_lam=lru_lam, sgu_norm=sgu_norm, sgu_w=sgu_w,
             sgu_b=sgu_b, out_norm=out_norm, w_out=w_out, w_ffn_in=w_ffn_in, w_ffn_out=w_ffn_out)
    B, S, _ = x.shape
    tctx = ctx.shape[1]
    mod_rows = ((B + 1 + SUBLANE - 1) // SUBLANE) * SUBLANE
    cc = jnp.concatenate([c, c_ctx[None], jnp.zeros((mod_rows - B - 1, D_MODEL), F32)], axis=0)
    mod_all = _ada_call(cc, w_ada, b_ada)
    tabs_l, tabs_c = _rope_tables(S, tctx)
    fn = final_norm[None]

    h_ctx = ctx
    for l in range(DEPTH):
        last = l == DEPTH - 1
        lw = _prep_layer(l, p)
        mod_l = mod_all[l, :B].reshape(B, 6, D_MODEL)
        mod_c = mod_all[l, B:B + 1].reshape(1, 6, D_MODEL)

        q_l, k_l, v_l, xr_l, gr_l, sp_l = _inproj_call(x, mod_l, True, tabs_l, lw, 512)
        q_c, k_c, v_c, xr_c, gr_c, sp_c = _inproj_call(h_ctx, mod_c, False, tabs_c, lw, 256)
        rec_c, rec_l = _lru_call(xr_c, xr_l, gr_c, gr_l, lw)
        att_l = _attn_call(q_l, [(k_c, v_c), (k_l, v_l)], lw["g_att"], 256)
        x = _out_call(x, att_l, rec_l, sp_l, mod_l, True, lw, fn, last, 512)
        if not last:
            att_c = _attn_call(q_c, [(k_c, v_c)], lw["g_att"], 256)
            h_ctx = _out_call(h_ctx, att_c, rec_c, sp_c, mod_c, False, lw, fn, False, 256)
    return x
```

```python
import functools
import math

import jax
import jax.numpy as jnp
import numpy as np
from jax import lax
from jax.experimental import pallas as pl
from jax.experimental.pallas import tpu as pltpu

F32 = jnp.float32
BF16 = jnp.bfloat16

D_MODEL = 1024
DEPTH = 2
GRID_W = 64
EPS = 1e-6
ROPE_BASE = 10000.0
HEADS = 8
NOPE = 64
ROPE = 32
V_DIM = 64
Q_LORA = 256
KV_LORA = 128
LRU_WIDTH = 256
LRU_HEADS = 4
CONV_W = 4
LRU_C = 8.0
SGU_GROUPS = 4
SGU_WIDTH = 256
SGU_GROUP_DIM = SGU_WIDTH // SGU_GROUPS
CHUNK = 128
D_FF = 2816
MLA_WIDTH = HEADS * V_DIM
V_EXT = V_DIM + 16

LANE = 128
SUBLANE = 8
HEAD_PAD = LANE
VMEM_LIMIT = 56 * 1024 * 1024

IN_EXT = 1536
C_QA, C_KVA, C_KR, C_XR, C_GR, C_SU, C_SV = 0, 256, 384, 512, 768, 1024, 1280
FF_CHUNK = 1408


def _rms(x, g):
    ms = jnp.mean(x * x, axis=-1, keepdims=True)
    return x * lax.rsqrt(ms + EPS) * g


def _dot(a, b):
    return jnp.dot(a, b, preferred_element_type=F32)


def _const_spec(shape):
    zeros = (0,) * len(shape)
    return pl.BlockSpec(shape, lambda *_: zeros, pipeline_mode=pl.Buffered(1))


ADA_TILE = 512


def _ada_kernel(c_ref, w_ref, b_ref, o_ref):
    s = jax.nn.silu(c_ref[...]).astype(BF16)
    o_ref[0] = _dot(s, w_ref[0].astype(BF16)) + b_ref[0]


def _ada_call(cc, w_ada, b_ada):
    rows = cc.shape[0]
    n = w_ada.shape[-1]
    return pl.pallas_call(
        _ada_kernel,
        out_shape=jax.ShapeDtypeStruct((DEPTH, rows, n), F32),
        grid=(DEPTH, n // ADA_TILE),
        in_specs=[
            pl.BlockSpec((rows, D_MODEL), lambda l, j: (0, 0)),
            pl.BlockSpec((1, D_MODEL, ADA_TILE), lambda l, j: (l, 0, j)),
            pl.BlockSpec((1, 1, ADA_TILE), lambda l, j: (l, 0, j)),
        ],
        out_specs=pl.BlockSpec((1, rows, ADA_TILE), lambda l, j: (l, 0, j)),
        compiler_params=pltpu.CompilerParams(
            dimension_semantics=("arbitrary", "arbitrary"), vmem_limit_bytes=VMEM_LIMIT),
        name="ada_mod",
    )(cc, w_ada, b_ada.reshape(DEPTH, 1, n))


def _inproj_kernel(x_ref, mod_ref, n1_ref, win_ref, qan_ref, wq_ref, kvn_ref, wkv_ref, wvt_ref,
                   cq_ref, sq_ref, tk_ref, wcat_ref, sbias_ref, sgn_ref, gmat_ref, gsp_ref,
                   q_ref, k_ref, v_ref, xr_ref, gr_ref, sp_ref, *, tm):
    x = x_ref[0]
    mod = mod_ref[0]
    h = _rms(x, n1_ref[...]) * (1.0 + mod[1:2]) + mod[0:1]
    z = _dot(h.astype(BF16), win_ref[...])

    qn = _rms(z[:, C_QA:C_QA + Q_LORA], qan_ref[...]).astype(BF16)
    qq = _dot(qn, wq_ref[...])
    cq = cq_ref[...]
    sq = sq_ref[...]
    rot0 = HEADS * HEAD_PAD
    for hd in range(HEADS):
        lo = hd * HEAD_PAD
        qh = qq[:, lo:lo + HEAD_PAD] * cq + qq[:, rot0 + lo:rot0 + lo + HEAD_PAD] * sq
        q_ref[0, hd] = qh.astype(BF16)

    kvn = _rms(z[:, C_KVA:C_KVA + KV_LORA], kvn_ref[...]).astype(BF16)
    t = z[:, C_KR:C_KR + LANE] * tk_ref[...]
    kro = t + pltpu.roll(t, LANE - ROPE, axis=1)
    lhs = jnp.concatenate([kvn, kro.astype(BF16)], axis=1)
    kk = _dot(lhs, wkv_ref[...])
    for hd in range(HEADS):
        k_ref[0, hd] = kk[:, hd * HEAD_PAD:(hd + 1) * HEAD_PAD].astype(BF16)
    vt = lax.dot_general(wvt_ref[...], kvn, (((1,), (1,)), ((), ())), preferred_element_type=F32)
    tail_row = lax.broadcasted_iota(jnp.int32, (V_EXT - V_DIM, tm), 0)
    tail = jnp.where(tail_row == 0, 1.0, 0.0).astype(BF16)
    for hd in range(HEADS):
        v_ref[0, hd] = jnp.concatenate([vt[hd * V_DIM:(hd + 1) * V_DIM].astype(BF16), tail], axis=0)

    xr_ref[0] = z[:, C_XR:C_XR + LRU_WIDTH]
    gr_ref[0] = z[:, C_GR:C_GR + LRU_WIDTH]

    u = jax.nn.gelu(z[:, C_SU:C_SU + SGU_WIDTH])
    vg = jax.nn.gelu(z[:, C_SV:C_SV + SGU_WIDTH])
    v2 = vg * vg
    v2_hi = v2.astype(BF16)
    v2_lo = (v2 - v2_hi.astype(F32)).astype(BF16)
    gmat = gmat_ref[...]
    gms = _dot(v2_hi, gmat) + _dot(v2_lo, gmat)
    vb = (vg * lax.rsqrt(gms + EPS) * sgn_ref[...]).astype(BF16)
    grp = lax.broadcasted_iota(jnp.int32, (CHUNK, SGU_WIDTH), 1) // SGU_GROUP_DIM
    zero = jnp.zeros((CHUNK, SGU_WIDTH), BF16)
    wcat = wcat_ref[...]
    sbias = sbias_ref[...]
    parts = []
    for c in range(tm // CHUNK):
        vc = vb[c * CHUNK:(c + 1) * CHUNK]
        rhs = jnp.concatenate([jnp.where(grp == g, vc, zero) for g in range(SGU_GROUPS)], axis=0)
        s = _dot(wcat, rhs) + sbias
        parts.append(u[c * CHUNK:(c + 1) * CHUNK] * s)
    sp = jnp.concatenate(parts, axis=0) if len(parts) > 1 else parts[0]
    sp_ref[0] = _rms(sp, gsp_ref[...]).astype(BF16)


def _inproj_call(x, mod, per_batch_mod, tabs, lw, tm):
    B, T, _ = x.shape
    cq, sq, tk = tabs
    mod_map = (lambda b, t: (b, 0, 0)) if per_batch_mod else (lambda b, t: (0, 0, 0))
    tab_spec = pl.BlockSpec((tm, LANE), lambda b, t: (t, 0))
    in_specs = [
        pl.BlockSpec((1, tm, D_MODEL), lambda b, t: (b, t, 0)),
        pl.BlockSpec((1, 6, D_MODEL), mod_map),
        _const_spec((1, D_MODEL)),
        _const_spec((D_MODEL, IN_EXT)),
        _const_spec((1, Q_LORA)),
        _const_spec((Q_LORA, 2 * HEADS * HEAD_PAD)),
        _const_spec((1, KV_LORA)),
        _const_spec((2 * LANE, HEADS * HEAD_PAD)),
        _const_spec((MLA_WIDTH, KV_LORA)),
        tab_spec, tab_spec, tab_spec,
        _const_spec((CHUNK, SGU_GROUPS * CHUNK)),
        _const_spec((CHUNK, SGU_WIDTH)),
        _const_spec((1, SGU_WIDTH)),
        _const_spec((SGU_WIDTH, SGU_WIDTH)),
        _const_spec((1, SGU_WIDTH)),
    ]
    out_shape = [
        jax.ShapeDtypeStruct((B, HEADS, T, HEAD_PAD), BF16),
        jax.ShapeDtypeStruct((B, HEADS, T, HEAD_PAD), BF16),
        jax.ShapeDtypeStruct((B, HEADS, V_EXT, T), BF16),
        jax.ShapeDtypeStruct((B, T, LRU_WIDTH), F32),
        jax.ShapeDtypeStruct((B, T, LRU_WIDTH), F32),
        jax.ShapeDtypeStruct((B, T, SGU_WIDTH), BF16),
    ]
    out_specs = [
        pl.BlockSpec((1, HEADS, tm, HEAD_PAD), lambda b, t: (b, 0, t, 0)),
        pl.BlockSpec((1, HEADS, tm, HEAD_PAD), lambda b, t: (b, 0, t, 0)),
        pl.BlockSpec((1, HEADS, V_EXT, tm), lambda b, t: (b, 0, 0, t)),
        pl.BlockSpec((1, tm, LRU_WIDTH), lambda b, t: (b, t, 0)),
        pl.BlockSpec((1, tm, LRU_WIDTH), lambda b, t: (b, t, 0)),
        pl.BlockSpec((1, tm, SGU_WIDTH), lambda b, t: (b, t, 0)),
    ]
    return pl.pallas_call(
        functools.partial(_inproj_kernel, tm=tm),
        out_shape=out_shape,
        grid=(B, T // tm),
        in_specs=in_specs,
        out_specs=out_specs,
        compiler_params=pltpu.CompilerParams(
            dimension_semantics=("parallel", "parallel"), vmem_limit_bytes=VMEM_LIMIT),
        name="inproj",
    )(x, mod, lw["norm1"], lw["w_in"], lw["q_a_norm"], lw["w_q"], lw["kv_a_norm"], lw["w_kv"], lw["w_vt"],
      cq, sq, tk, lw["sgu_wcat"], lw["sgu_bias"], lw["sgu_norm"], lw["gmat"], lw["g_sp"])


GATE_ROWS = 256
PAD_ROWS = SUBLANE
SCAN_UNROLL = 4


def _sigmoid(x):
    return 0.5 * jnp.tanh(0.5 * x) + 0.5


def _tile_scan(a, b, h_prev, row, reverse):
    for k in (1, 2, 4):
        if reverse:
            keep = row < SUBLANE - k
            sh = SUBLANE - k
        else:
            keep = row >= k
            sh = k
        a_s = jnp.where(keep, pltpu.roll(a, sh, axis=0), 1.0)
        b_s = jnp.where(keep, pltpu.roll(b, sh, axis=0), 0.0)
        b = a * b_s + b
        a = a * a_s
    h = a * h_prev + b
    last = h[0:1] if reverse else h[SUBLANE - 1:SUBLANE]
    return h, jnp.broadcast_to(last, h.shape)


def _lru_kernel(xrc_ref, xrl_ref, grc_ref, grl_ref, cw_ref, cb_ref, wg_ref, bg_ref, lam_ref,
                grec_ref, outc_ref, outl_ref, pad_ref, xc_ref, a_ref, b_ref, hs_ref, *, tc, tl):
    cw = cw_ref[...]
    cb = cb_ref[...]
    ttot = tc + tl

    def conv(src_ref, n, dst0):
        pad_ref[pl.ds(0, PAD_ROWS), :] = jnp.zeros((PAD_ROWS, LRU_WIDTH), F32)
        pad_ref[pl.ds(PAD_ROWS, n), :] = src_ref[0]
        pad_ref[pl.ds(PAD_ROWS + n, PAD_ROWS), :] = jnp.zeros((PAD_ROWS, LRU_WIDTH), F32)
        y = cb
        for j in range(CONV_W):
            y = y + cw[j:j + 1] * pad_ref[pl.ds(PAD_ROWS - CONV_W // 2 + j, n), :]
        xc_ref[pl.ds(dst0, n), :] = y

    conv(xrc_ref, tc, 0)
    conv(xrl_ref, tl, tc)

    nsp = -LRU_C * jax.nn.softplus(-lam_ref[...])
    wg = wg_ref[...]
    bg = bg_ref[...]
    tiles_per_chunk = GATE_ROWS // SUBLANE
    for c in range(ttot // GATE_ROWS):
        xcc = xc_ref[pl.ds(c * GATE_ROWS, GATE_ROWS), :]
        g = _dot(xcc.astype(BF16), wg) + bg
        for d in range(2):
            r = _sigmoid(g[:, (2 * d) * LRU_WIDTH:(2 * d + 1) * LRU_WIDTH])
            i = _sigmoid(g[:, (2 * d + 1) * LRU_WIDTH:(2 * d + 2) * LRU_WIDTH])
            log_a = nsp[d:d + 1] * r
            a = jnp.exp(log_a)
            bb = jnp.sqrt(-jnp.tanh(log_a) * (a * a + 1.0)) * (i * xcc)
            a_ref[d, pl.ds(c * tiles_per_chunk, tiles_per_chunk)] = a.reshape(
                tiles_per_chunk, SUBLANE, LRU_WIDTH)
            b_ref[d, pl.ds(c * tiles_per_chunk, tiles_per_chunk)] = bb.reshape(
                tiles_per_chunk, SUBLANE, LRU_WIDTH)

    row = lax.broadcasted_iota(jnp.int32, (SUBLANE, LRU_WIDTH), 0)
    nc = tc // SUBLANE
    nl = tl // SUBLANE

    def run(first_f, first_r, n, hf, hr):
        def body(i, carry):
            hf, hr = carry
            for j in range(SCAN_UNROLL):
                tf = first_f + i * SCAN_UNROLL + j
                tr = first_r - i * SCAN_UNROLL - j
                of, hf = _tile_scan(a_ref[0, tf], b_ref[0, tf], hf, row, False)
                orv, hr = _tile_scan(a_ref[1, tr], b_ref[1, tr], hr, row, True)
                hs_ref[0, tf] = of
                hs_ref[1, tr] = orv
            return hf, hr
        return lax.fori_loop(0, n // SCAN_UNROLL, body, (hf, hr))

    zero = jnp.zeros((SUBLANE, LRU_WIDTH), F32)
    hf, hr = run(0, nc - 1, nc, zero, zero)
    run(nc, nc + nl - 1, nl, hf, hr)

    grec = grec_ref[...]
    for c in range(ttot // GATE_ROWS):
        sl = pl.ds(c * tiles_per_chunk, tiles_per_chunk)
        hsum = (hs_ref[0, sl] + hs_ref[1, sl]).reshape(GATE_ROWS, LRU_WIDTH)
        r0 = c * GATE_ROWS
        if r0 < tc:
            gate = grc_ref[0, pl.ds(r0, GATE_ROWS), :]
        else:
            gate = grl_ref[0, pl.ds(r0 - tc, GATE_ROWS), :]
        y = _rms(hsum * jax.nn.gelu(gate), grec).astype(BF16)
        if r0 < tc:
            outc_ref[0, pl.ds(r0, GATE_ROWS), :] = y
        else:
            outl_ref[0, pl.ds(r0 - tc, GATE_ROWS), :] = y


def _lru_call(xr_c, xr_l, gr_c, gr_l, lw):
    B, tc, _ = xr_c.shape
    tl = xr_l.shape[1]
    ttot = tc + tl
    assert tc % GATE_ROWS == 0 and tl % GATE_ROWS == 0
    seq = lambda n: pl.BlockSpec((1, n, LRU_WIDTH), lambda b: (b, 0, 0))
    return pl.pallas_call(
        functools.partial(_lru_kernel, tc=tc, tl=tl),
        out_shape=[jax.ShapeDtypeStruct((B, tc, LRU_WIDTH), BF16),
                   jax.ShapeDtypeStruct((B, tl, LRU_WIDTH), BF16)],
        grid=(B,),
        in_specs=[seq(tc), seq(tl), seq(tc), seq(tl),
                  _const_spec((CONV_W, LRU_WIDTH)), _const_spec((1, LRU_WIDTH)),
                  _const_spec((LRU_WIDTH, 4 * LRU_WIDTH)), _const_spec((1, 4 * LRU_WIDTH)),
                  _const_spec((2, LRU_WIDTH)), _const_spec((1, LRU_WIDTH))],
        out_specs=[seq(tc), seq(tl)],
        scratch_shapes=[
            pltpu.VMEM((tl + 2 * PAD_ROWS, LRU_WIDTH), F32),
            pltpu.VMEM((ttot, LRU_WIDTH), F32),
            pltpu.VMEM((2, ttot // SUBLANE, SUBLANE, LRU_WIDTH), F32),
            pltpu.VMEM((2, ttot // SUBLANE, SUBLANE, LRU_WIDTH), F32),
            pltpu.VMEM((2, ttot // SUBLANE, SUBLANE, LRU_WIDTH), F32),
        ],
        compiler_params=pltpu.CompilerParams(
            dimension_semantics=("parallel",), vmem_limit_bytes=VMEM_LIMIT),
        name="lru",
    )(xr_c, xr_l, gr_c, gr_l, lw["conv_w"], lw["conv_b"], lw["w_gates"], lw["b_gates"],
      lw["lru_lam"], lw["g_rec"])


RED_ROWS = 64


def _col_reduce(x, op):
    rows, cols = x.shape
    part = op(x.reshape(rows // RED_ROWS, RED_ROWS, cols), axis=0)
    return op(part, axis=0, keepdims=True)


def _attn_kernel(*refs, nseg):
    q_ref = refs[0]
    kv_refs = refs[1:1 + 2 * nseg]
    g_ref = refs[1 + 2 * nseg]
    o_ref = refs[2 + 2 * nseg]
    s_scr = refs[3 + 2 * nseg:5 + 2 * nseg]
    o_scr = refs[5 + 2 * nseg]
    seg_len = [kv_refs[2 * s].shape[2] for s in range(nseg)]
    seg_off = [sum(seg_len[:s]) for s in range(nseg)]

    def scores_into(hd, slot):
        q = q_ref[0, hd]
        m = None
        for s in range(nseg):
            sc = lax.dot_general(kv_refs[2 * s][0, hd], q, (((1,), (1,)), ((), ())),
                                 preferred_element_type=F32)
            s_scr[slot][pl.ds(seg_off[s], seg_len[s]), :] = sc
            ms = _col_reduce(sc, jnp.max)
            m = ms if m is None else jnp.maximum(m, ms)
        return m

    def softmax_pv(hd, slot, m):
        o = None
        for s in range(nseg):
            rows = pl.ds(seg_off[s], seg_len[s])
            p = jnp.exp2(s_scr[slot][rows, :] - m).astype(BF16)
            po = _dot(kv_refs[2 * s + 1][0, hd], p)
            o = po if o is None else o + po
        o_scr[hd] = o[:V_DIM] * (1.0 / o[V_DIM:V_DIM + 1])

    def trip(i, m0):
        h0 = 2 * i
        m1 = scores_into(h0 + 1, 1)
        softmax_pv(h0, 0, m0)
        m2 = scores_into(h0 + 2, 0)
        softmax_pv(h0 + 1, 1, m1)
        return m2

    m6 = lax.fori_loop(0, HEADS // 2 - 1, trip, scores_into(0, 0))
    m7 = scores_into(HEADS - 1, 1)
    softmax_pv(HEADS - 2, 0, m6)
    softmax_pv(HEADS - 1, 1, m7)
    att = o_scr[...].reshape(MLA_WIDTH, o_scr.shape[2]).T
    o_ref[0] = _rms(att, g_ref[...]).astype(BF16)


def _attn_call(q, segs, g_att, tq):
    B, _, S, _ = q.shape
    nseg = len(segs)
    ttot = sum(k.shape[2] for k, _ in segs)
    in_specs = [pl.BlockSpec((1, HEADS, tq, HEAD_PAD), lambda b, t: (b, 0, t, 0))]
    args = [q]
    for k, v in segs:
        tn = k.shape[2]
        in_specs.append(pl.BlockSpec((1, HEADS, tn, HEAD_PAD), lambda b, t: (b, 0, 0, 0)))
        in_specs.append(pl.BlockSpec((1, HEADS, V_EXT, tn), lambda b, t: (b, 0, 0, 0)))
        args += [k, v]
    in_specs.append(_const_spec((1, MLA_WIDTH)))
    args.append(g_att)
    return pl.pallas_call(
        functools.partial(_attn_kernel, nseg=nseg),
        out_shape=jax.ShapeDtypeStruct((B, S, MLA_WIDTH), BF16),
        grid=(B, S // tq),
        in_specs=in_specs,
        out_specs=pl.BlockSpec((1, tq, MLA_WIDTH), lambda b, t: (b, t, 0)),
        scratch_shapes=[pltpu.VMEM((ttot, tq), F32), pltpu.VMEM((ttot, tq), F32),
                        pltpu.VMEM((HEADS, V_DIM, tq), F32)],
        compiler_params=pltpu.CompilerParams(
            dimension_semantics=("parallel", "arbitrary"), vmem_limit_bytes=VMEM_LIMIT),
        name="attn",
    )(*args)


def _out_kernel(x_ref, att_ref, rec_ref, sp_ref, mod_ref, n2_ref, wo_ref, wfi_ref, wfo_ref,
                fn_ref, o_ref, *, final):
    mod = mod_ref[0]
    y = _dot(att_ref[0], wo_ref[pl.ds(0, MLA_WIDTH), :])
    y = y + _dot(rec_ref[0], wo_ref[pl.ds(MLA_WIDTH, LRU_WIDTH), :])
    y = y + _dot(sp_ref[0], wo_ref[pl.ds(MLA_WIDTH + LRU_WIDTH, SGU_WIDTH), :])
    x1 = x_ref[0] + mod[2:3] * y
    h2 = (_rms(x1, n2_ref[...]) * (1.0 + mod[4:5]) + mod[3:4]).astype(BF16)
    acc = None
    for j in range(D_FF // FF_CHUNK):
        g = _dot(h2, wfi_ref[:, pl.ds(j * FF_CHUNK, FF_CHUNK)])
        u = _dot(h2, wfi_ref[:, pl.ds(D_FF + j * FF_CHUNK, FF_CHUNK)])
        act = (jax.nn.silu(g) * u).astype(BF16)
        part = _dot(act, wfo_ref[pl.ds(j * FF_CHUNK, FF_CHUNK), :])
        acc = part if acc is None else acc + part
    x2 = x1 + mod[5:6] * acc
    if final:
        x2 = _rms(x2, fn_ref[...])
    o_ref[0] = x2


def _out_call(x, att, rec, sp, mod, per_batch_mod, lw, final_norm, final, tm):
    B, T, _ = x.shape
    mod_map = (lambda b, t: (b, 0, 0)) if per_batch_mod else (lambda b, t: (0, 0, 0))
    tok = lambda w: pl.BlockSpec((1, tm, w), lambda b, t: (b, t, 0))
    return pl.pallas_call(
        functools.partial(_out_kernel, final=final),
        out_shape=jax.ShapeDtypeStruct((B, T, D_MODEL), F32),
        grid=(B, T // tm),
        in_specs=[tok(D_MODEL), tok(MLA_WIDTH), tok(LRU_WIDTH), tok(SGU_WIDTH),
                  pl.BlockSpec((1, 6, D_MODEL), mod_map),
                  _const_spec((1, D_MODEL)),
                  _const_spec((D_MODEL, D_MODEL)),
                  _const_spec((D_MODEL, 2 * D_FF)),
                  _const_spec((D_FF, D_MODEL)),
                  _const_spec((1, D_MODEL))],
        out_specs=tok(D_MODEL),
        compiler_params=pltpu.CompilerParams(
            dimension_semantics=("parallel", "parallel"), vmem_limit_bytes=VMEM_LIMIT),
        name="out_ffn",
    )(x, att, rec, sp, mod, lw["norm2"], lw["w_out"], lw["w_ffn_in"], lw["w_ffn_out"], final_norm)


def _rot_cols(w):
    q = ROPE // 4
    return jnp.concatenate([-w[..., q:2 * q], w[..., 0:q], -w[..., 3 * q:4 * q], w[..., 2 * q:3 * q]], axis=-1)


def _prep_layer(l, p):
    w_in = p["w_in"][l]
    splits = np.cumsum([Q_LORA, KV_LORA, ROPE, LRU_WIDTH, LRU_WIDTH, SGU_WIDTH])
    qa, kva, kr, xr, gr, su, sv = jnp.split(w_in, splits, axis=1)
    kr_blk = jnp.concatenate([kr, _rot_cols(kr), jnp.zeros((D_MODEL, LANE - 2 * ROPE), F32)], axis=1)
    w_in_ext = jnp.concatenate([qa, kva, kr_blk, xr, gr, su, sv], axis=1).astype(BF16)

    wq = p["w_q_b"][l].reshape(Q_LORA, HEADS, NOPE + ROPE)
    zpad = jnp.zeros((Q_LORA, HEADS, HEAD_PAD - NOPE - ROPE), F32)
    wq_plain = jnp.concatenate([wq, zpad], axis=-1).reshape(Q_LORA, HEADS * HEAD_PAD)
    wq_rot = jnp.concatenate([jnp.zeros((Q_LORA, HEADS, NOPE), F32), _rot_cols(wq[..., NOPE:]), zpad],
                             axis=-1).reshape(Q_LORA, HEADS * HEAD_PAD)
    w_q = jnp.concatenate([wq_plain, wq_rot], axis=1).astype(BF16)

    wkv = p["w_kv_b"][l].reshape(KV_LORA, HEADS, NOPE + V_DIM)
    wk = jnp.concatenate([wkv[..., :NOPE], jnp.zeros((KV_LORA, HEADS, HEAD_PAD - NOPE), F32)],
                         axis=-1).reshape(KV_LORA, HEADS * HEAD_PAD)
    wv = wkv[..., NOPE:].reshape(KV_LORA, MLA_WIDTH)
    place = jnp.zeros((LANE, HEADS, HEAD_PAD), F32)
    place = place.at[jnp.arange(ROPE), :, NOPE + jnp.arange(ROPE)].set(1.0)
    w_kv = jnp.concatenate([wk, place.reshape(LANE, HEADS * HEAD_PAD)], axis=0).astype(BF16)
    w_vt = wv.T.astype(BF16)

    def block_diag(w):
        hh, bi, bj = w.shape
        eye = jnp.eye(hh, dtype=F32)
        return (eye[:, None, :, None] * w[:, :, None, :]).reshape(hh * bi, hh * bj)

    w_gates = jnp.concatenate([block_diag(p["lru_w_r"][l, 0]), block_diag(p["lru_w_i"][l, 0]),
                               block_diag(p["lru_w_r"][l, 1]), block_diag(p["lru_w_i"][l, 1])],
                              axis=1).astype(BF16)
    b_gates = jnp.concatenate([p["lru_b_r"][l, 0], p["lru_b_i"][l, 0],
                               p["lru_b_r"][l, 1], p["lru_b_i"][l, 1]])[None]

    sgu_wcat = jnp.transpose(p["sgu_w"][l], (1, 0, 2)).reshape(CHUNK, SGU_GROUPS * CHUNK).astype(BF16)
    sgu_bias = jnp.repeat(p["sgu_b"][l].T, SGU_GROUP_DIM, axis=1)
    gidx = jnp.arange(SGU_WIDTH) // SGU_GROUP_DIM
    gmat = ((gidx[:, None] == gidx[None, :]).astype(F32) / SGU_GROUP_DIM).astype(BF16)

    on = p["out_norm"][l]
    return {
        "norm1": p["norm1"][l][None], "norm2": p["norm2"][l][None],
        "w_in": w_in_ext, "q_a_norm": p["q_a_norm"][l][None], "w_q": w_q,
        "kv_a_norm": p["kv_a_norm"][l][None], "w_kv": w_kv, "w_vt": w_vt,
        "conv_w": p["conv_w"][l], "conv_b": p["conv_b"][l][None],
        "w_gates": w_gates, "b_gates": b_gates, "lru_lam": p["lru_lam"][l],
        "sgu_wcat": sgu_wcat, "sgu_bias": sgu_bias, "sgu_norm": p["sgu_norm"][l][None], "gmat": gmat,
        "g_att": on[None, :MLA_WIDTH], "g_rec": on[None, MLA_WIDTH:MLA_WIDTH + LRU_WIDTH],
        "g_sp": on[None, MLA_WIDTH + LRU_WIDTH:],
        "w_out": p["w_out"][l].astype(BF16), "w_ffn_in": p["w_ffn_in"][l].astype(BF16),
        "w_ffn_out": p["w_ffn_out"][l].astype(BF16),
    }


def _rope_tables(seq_len, ctx_len):
    rows = seq_len // GRID_W
    row = jnp.repeat(jnp.arange(rows, dtype=F32), GRID_W)
    col = jnp.tile(jnp.arange(GRID_W, dtype=F32), rows)
    half = ROPE // 2
    freq = ROPE_BASE ** (-jnp.arange(0, half, 2, dtype=F32) / half)
    ar = row[:, None] * freq
    ac = col[:, None] * freq
    ang = jnp.concatenate([ar, ar, ac, ac], axis=-1)
    cos, sin = jnp.cos(ang), jnp.sin(ang)
    qs = math.log2(math.e) / math.sqrt(NOPE + ROPE)

    def q_tabs(c, s, n):
        cq = jnp.concatenate([jnp.ones((n, NOPE), F32), c, jnp.zeros((n, HEAD_PAD - NOPE - ROPE), F32)], axis=1)
        sq = jnp.concatenate([jnp.zeros((n, NOPE), F32), s, jnp.zeros((n, HEAD_PAD - NOPE - ROPE), F32)], axis=1)
        tk = jnp.concatenate([c, s, jnp.zeros((n, LANE - 2 * ROPE), F32)], axis=1)
        return cq * qs, sq * qs, tk

    lat = q_tabs(cos, sin, seq_len)
    ctx = q_tabs(jnp.ones((ctx_len, ROPE), F32), jnp.zeros((ctx_len, ROPE), F32), ctx_len)
    return lat, ctx


def kernel(x, c, ctx, c_ctx, norm1, norm2, w_ada, b_ada, w_in, q_a_norm, w_q_b, kv_a_norm, w_kv_b, conv_w, conv_b, lru_w_r, lru_b_r, lru_w_i, lru_b_i, lru_lam, sgu_norm, sgu_w, sgu_b, out_norm, w_out, w_ffn_in, w_ffn_out, final_norm):
    p = dict(norm1=norm1, norm2=norm2, w_in=w_in, q_a_norm=q_a_norm, w_q_b=w_q_b, kv_a_norm=kv_a_norm,
             w_kv_b=w_kv_b, conv_w=conv_w, conv_b=conv_b, lru_w_r=lru_w_r, lru_b_r=lru_b_r,
             lru_w_i=lru_w_i, lru_b_i=lru_b_i, lru_lam=lru_lam, sgu_norm=sgu_norm, sgu_w=sgu_w,
             sgu_b=sgu_b, out_norm=out_norm, w_out=w_out, w_ffn_in=w_ffn_in, w_ffn_out=w_ffn_out)
    B, S, _ = x.shape
    tctx = ctx.shape[1]
    mod_rows = ((B + 1 + SUBLANE - 1) // SUBLANE) * SUBLANE
    cc = jnp.concatenate([c, c_ctx[None], jnp.zeros((mod_rows - B - 1, D_MODEL), F32)], axis=0)
    mod_all = _ada_call(cc, w_ada, b_ada)
    tabs_l, tabs_c = _rope_tables(S, tctx)
    fn = final_norm[None]

    h_ctx = ctx
    for l in range(DEPTH):
        last = l == DEPTH - 1
        lw = _prep_layer(l, p)
        mod_l = mod_all[l, :B].reshape(B, 6, D_MODEL)
        mod_c = mod_all[l, B:B + 1].reshape(1, 6, D_MODEL)

        q_l, k_l, v_l, xr_l, gr_l, sp_l = _inproj_call(x, mod_l, True, tabs_l, lw, 512)
        q_c, k_c, v_c, xr_c, gr_c, sp_c = _inproj_call(h_ctx, mod_c, False, tabs_c, lw, 256)
        rec_c, rec_l = _lru_call(xr_c, xr_l, gr_c, gr_l, lw)
        att_l = _attn_call(q_l, [(k_c, v_c), (k_l, v_l)], lw["g_att"], 256)
        x = _out_call(x, att_l, rec_l, sp_l, mod_l, True, lw, fn, last, 512)
        if not last:
            att_c = _attn_call(q_c, [(k_c, v_c)], lw["g_att"], 256)
            h_ctx = _out_call(h_ctx, att_c, rec_c, sp_c, mod_c, False, lw, fn, False, 256)
    return x
```

```python
import functools
import math

import jax
import jax.numpy as jnp
import numpy as np
from jax import lax
from jax.experimental import pallas as pl
from jax.experimental.pallas import tpu as pltpu

F32 = jnp.float32
BF16 = jnp.bfloat16

D_MODEL = 1024
DEPTH = 2
GRID_W = 64
EPS = 1e-6
ROPE_BASE = 10000.0
HEADS = 8
NOPE = 64
ROPE = 32
V_DIM = 64
Q_LORA = 256
KV_LORA = 128
LRU_WIDTH = 256
LRU_HEADS = 4
CONV_W = 4
LRU_C = 8.0
SGU_GROUPS = 4
SGU_WIDTH = 256
SGU_GROUP_DIM = SGU_WIDTH // SGU_GROUPS
CHUNK = 128
D_FF = 2816
MLA_WIDTH = HEADS * V_DIM
V_EXT = V_DIM + 16

LANE = 128
SUBLANE = 8
HEAD_PAD = LANE
ROT_SHIFT = ROPE // 4
VMEM_LIMIT = 56 * 1024 * 1024

IN_EXT = 1536
C_QA, C_KVA, C_KR, C_XR, C_GR, C_SU, C_SV = 0, 256, 384, 512, 768, 1024, 1280
MXU_TILE = 256
FF_CHUNKS = ((0, 6 * MXU_TILE), (6 * MXU_TILE, 5 * MXU_TILE))


def _rms(x, g):
    ms = jnp.mean(x * x, axis=-1, keepdims=True)
    return x * lax.rsqrt(ms + EPS) * g


def _dot(a, b):
    return jnp.dot(a, b, preferred_element_type=F32)


def _const_spec(shape):
    zeros = (0,) * len(shape)
    return pl.BlockSpec(shape, lambda *_: zeros, pipeline_mode=pl.Buffered(1))


def _layer_spec(shape, l):
    zeros = (0,) * len(shape)
    return pl.BlockSpec((None,) + tuple(shape), lambda *_: (l,) + zeros, pipeline_mode=pl.Buffered(1))


def _mod_spec(l, row):
    if row is None:
        return pl.BlockSpec((None, None, 6, D_MODEL), lambda b, t: (l, b, 0, 0))
    return pl.BlockSpec((None, None, 6, D_MODEL), lambda b, t: (l, row, 0, 0))


ADA_TILE = 512


def _ada_kernel(c_ref, w_ref, b_ref, o_ref):
    s = jax.nn.silu(c_ref[...]).astype(BF16)
    o_ref[0] = _dot(s, w_ref[0].astype(BF16)) + b_ref[0]


def _ada_call(cc, w_ada, b_ada):
    rows = cc.shape[0]
    n = w_ada.shape[-1]
    return pl.pallas_call(
        _ada_kernel,
        out_shape=jax.ShapeDtypeStruct((DEPTH, rows, n), F32),
        grid=(DEPTH, n // ADA_TILE),
        in_specs=[
            pl.BlockSpec((rows, D_MODEL), lambda l, j: (0, 0)),
            pl.BlockSpec((1, D_MODEL, ADA_TILE), lambda l, j: (l, 0, j)),
            pl.BlockSpec((1, 1, ADA_TILE), lambda l, j: (l, 0, j)),
        ],
        out_specs=pl.BlockSpec((1, rows, ADA_TILE), lambda l, j: (l, 0, j)),
        compiler_params=pltpu.CompilerParams(
            dimension_semantics=("arbitrary", "arbitrary"), vmem_limit_bytes=VMEM_LIMIT),
        name="ada_mod",
    )(cc, w_ada, b_ada.reshape(DEPTH, 1, n))


def _inproj_kernel(x_ref, mod_ref, n1_ref, win_ref, qan_ref, wq_ref, kvn_ref, wkv_ref, wvt_ref,
                   cq_ref, sa_ref, sb_ref, tk_ref, wcat_ref, sbias_ref, sgn_ref, gmat_ref, gsp_ref,
                   q_ref, k_ref, v_ref, xr_ref, gr_ref, sp_ref, *, tm):
    x = x_ref[0]
    mod = mod_ref[...]
    h = _rms(x, n1_ref[...]) * (1.0 + mod[1:2]) + mod[0:1]
    z = _dot(h.astype(BF16), win_ref[...])

    qn = _rms(z[:, C_QA:C_QA + Q_LORA], qan_ref[...]).astype(BF16)
    qq = _dot(qn, wq_ref[...])
    cq = cq_ref[...]
    sa = sa_ref[...]
    sb = sb_ref[...]
    for hd in range(HEADS):
        qh = qq[:, hd * HEAD_PAD:(hd + 1) * HEAD_PAD]
        qh = (qh * cq + pltpu.roll(qh, LANE - ROT_SHIFT, axis=1) * sa
              + pltpu.roll(qh, ROT_SHIFT, axis=1) * sb)
        q_ref[0, hd] = qh.astype(BF16)

    kvn = _rms(z[:, C_KVA:C_KVA + KV_LORA], kvn_ref[...]).astype(BF16)
    t = z[:, C_KR:C_KR + LANE] * tk_ref[...]
    kro = t + pltpu.roll(t, LANE - ROPE, axis=1)
    lhs = jnp.concatenate([kvn, kro.astype(BF16)], axis=1)
    kk = _dot(lhs, wkv_ref[...])
    for hd in range(HEADS):
        k_ref[0, hd] = kk[:, hd * HEAD_PAD:(hd + 1) * HEAD_PAD].astype(BF16)
    vt = lax.dot_general(wvt_ref[...], kvn, (((1,), (1,)), ((), ())), preferred_element_type=F32)
    tail_row = lax.broadcasted_iota(jnp.int32, (V_EXT - V_DIM, tm), 0)
    tail = jnp.where(tail_row == 0, 1.0, 0.0).astype(BF16)
    for hd in range(HEADS):
        v_ref[0, hd] = jnp.concatenate([vt[hd * V_DIM:(hd + 1) * V_DIM].astype(BF16), tail], axis=0)

    xr_ref[0] = z[:, C_XR:C_XR + LRU_WIDTH]
    gr_ref[0] = z[:, C_GR:C_GR + LRU_WIDTH]

    u = jax.nn.gelu(z[:, C_SU:C_SU + SGU_WIDTH])
    vg = jax.nn.gelu(z[:, C_SV:C_SV + SGU_WIDTH])
    v2 = vg * vg
    v2_hi = v2.astype(BF16)
    v2_lo = (v2 - v2_hi.astype(F32)).astype(BF16)
    gmat = gmat_ref[...]
    gms = _dot(v2_hi, gmat) + _dot(v2_lo, gmat)
    vb = (vg * lax.rsqrt(gms + EPS) * sgn_ref[...]).astype(BF16)
    grp = lax.broadcasted_iota(jnp.int32, (CHUNK, SGU_WIDTH), 1) // SGU_GROUP_DIM
    zero = jnp.zeros((CHUNK, SGU_WIDTH), BF16)
    wcat = wcat_ref[...]
    sbias = sbias_ref[...]
    parts = []
    for c in range(tm // CHUNK):
        vc = vb[c * CHUNK:(c + 1) * CHUNK]
        rhs = jnp.concatenate([jnp.where(grp == g, vc, zero) for g in range(SGU_GROUPS)], axis=0)
        s = _dot(wcat, rhs) + sbias
        parts.append(u[c * CHUNK:(c + 1) * CHUNK] * s)
    sp = jnp.concatenate(parts, axis=0) if len(parts) > 1 else parts[0]
    sp_ref[0] = _rms(sp, gsp_ref[...]).astype(BF16)


def _inproj_call(x, mod, mod_row, tabs, lw, l, tm):
    B, T, _ = x.shape
    cq, sa, sb, tk = tabs
    tab_spec = pl.BlockSpec((tm, LANE), lambda b, t: (t, 0))
    W = functools.partial(_layer_spec, l=l)
    in_specs = [
        pl.BlockSpec((1, tm, D_MODEL), lambda b, t: (b, t, 0)),
        _mod_spec(l, mod_row),
        W((1, D_MODEL)),
        W((D_MODEL, IN_EXT)),
        W((1, Q_LORA)),
        W((Q_LORA, HEADS * HEAD_PAD)),
        W((1, KV_LORA)),
        W((2 * LANE, HEADS * HEAD_PAD)),
        W((MLA_WIDTH, KV_LORA)),
        tab_spec, tab_spec, tab_spec, tab_spec,
        W((CHUNK, SGU_GROUPS * CHUNK)),
        W((CHUNK, SGU_WIDTH)),
        W((1, SGU_WIDTH)),
        _const_spec((SGU_WIDTH, SGU_WIDTH)),
        W((1, SGU_WIDTH)),
    ]
    out_shape = [
        jax.ShapeDtypeStruct((B, HEADS, T, HEAD_PAD), BF16),
        jax.ShapeDtypeStruct((B, HEADS, T, HEAD_PAD), BF16),
        jax.ShapeDtypeStruct((B, HEADS, V_EXT, T), BF16),
        jax.ShapeDtypeStruct((B, T, LRU_WIDTH), F32),
        jax.ShapeDtypeStruct((B, T, LRU_WIDTH), F32),
        jax.ShapeDtypeStruct((B, T, SGU_WIDTH), BF16),
    ]
    out_specs = [
        pl.BlockSpec((1, HEADS, tm, HEAD_PAD), lambda b, t: (b, 0, t, 0)),
        pl.BlockSpec((1, HEADS, tm, HEAD_PAD), lambda b, t: (b, 0, t, 0)),
        pl.BlockSpec((1, HEADS, V_EXT, tm), lambda b, t: (b, 0, 0, t)),
        pl.BlockSpec((1, tm, LRU_WIDTH), lambda b, t: (b, t, 0)),
        pl.BlockSpec((1, tm, LRU_WIDTH), lambda b, t: (b, t, 0)),
        pl.BlockSpec((1, tm, SGU_WIDTH), lambda b, t: (b, t, 0)),
    ]
    return pl.pallas_call(
        functools.partial(_inproj_kernel, tm=tm),
        out_shape=out_shape,
        grid=(B, T // tm),
        in_specs=in_specs,
        out_specs=out_specs,
        compiler_params=pltpu.CompilerParams(
            dimension_semantics=("parallel", "parallel"), vmem_limit_bytes=VMEM_LIMIT),
        name="inproj",
    )(x, mod, lw["norm1"], lw["w_in"], lw["q_a_norm"], lw["w_q"], lw["kv_a_norm"], lw["w_kv"], lw["w_vt"],
      cq, sa, sb, tk, lw["sgu_wcat"], lw["sgu_bias"], lw["sgu_norm"], lw["gmat"], lw["g_sp"])


GATE_ROWS = 256
PAD_ROWS = SUBLANE
SCAN_UNROLL = 4


def _sigmoid(x):
    return 0.5 * jnp.tanh(0.5 * x) + 0.5


def _tile_scan(a, b, h_prev, row, reverse):
    for k in (1, 2, 4):
        if reverse:
            keep = row < SUBLANE - k
            sh = SUBLANE - k
        else:
            keep = row >= k
            sh = k
        a_s = jnp.where(keep, pltpu.roll(a, sh, axis=0), 1.0)
        b_s = jnp.where(keep, pltpu.roll(b, sh, axis=0), 0.0)
        b = a * b_s + b
        a = a * a_s
    h = a * h_prev + b
    last = h[0:1] if reverse else h[SUBLANE - 1:SUBLANE]
    return h, jnp.broadcast_to(last, h.shape)


def _lru_kernel(xrc_ref, xrl_ref, grc_ref, grl_ref, cw_ref, cb_ref, wg_ref, bg_ref, lam_ref,
                grec_ref, outc_ref, outl_ref, pad_ref, xc_ref, a_ref, b_ref, hs_ref, *, tc, tl):
    cw = cw_ref[...]
    cb = cb_ref[...]
    ttot = tc + tl

    def conv(src_ref, n, dst0):
        pad_ref[pl.ds(0, PAD_ROWS), :] = jnp.zeros((PAD_ROWS, LRU_WIDTH), F32)
        pad_ref[pl.ds(PAD_ROWS, n), :] = src_ref[0]
        pad_ref[pl.ds(PAD_ROWS + n, PAD_ROWS), :] = jnp.zeros((PAD_ROWS, LRU_WIDTH), F32)
        y = cb
        for j in range(CONV_W):
            y = y + cw[j:j + 1] * pad_ref[pl.ds(PAD_ROWS - CONV_W // 2 + j, n), :]
        xc_ref[pl.ds(dst0, n), :] = y

    conv(xrc_ref, tc, 0)
    conv(xrl_ref, tl, tc)

    nsp = -LRU_C * jax.nn.softplus(-lam_ref[...])
    wg = wg_ref[...]
    bg = bg_ref[...]
    tiles_per_chunk = GATE_ROWS // SUBLANE
    for c in range(ttot // GATE_ROWS):
        xcc = xc_ref[pl.ds(c * GATE_ROWS, GATE_ROWS), :]
        g = _dot(xcc.astype(BF16), wg) + bg
        for d in range(2):
            r = _sigmoid(g[:, (2 * d) * LRU_WIDTH:(2 * d + 1) * LRU_WIDTH])
            i = _sigmoid(g[:, (2 * d + 1) * LRU_WIDTH:(2 * d + 2) * LRU_WIDTH])
            log_a = nsp[d:d + 1] * r
            a = jnp.exp(log_a)
            bb = jnp.sqrt(-jnp.tanh(log_a) * (a * a + 1.0)) * (i * xcc)
            a_ref[d, pl.ds(c * tiles_per_chunk, tiles_per_chunk)] = a.reshape(
                tiles_per_chunk, SUBLANE, LRU_WIDTH)
            b_ref[d, pl.ds(c * tiles_per_chunk, tiles_per_chunk)] = bb.reshape(
                tiles_per_chunk, SUBLANE, LRU_WIDTH)

    row = lax.broadcasted_iota(jnp.int32, (SUBLANE, LRU_WIDTH), 0)
    nc = tc // SUBLANE
    nl = tl // SUBLANE

    def run(first_f, first_r, n, hf, hr):
        def body(i, carry):
            hf, hr = carry
            for j in range(SCAN_UNROLL):
                tf = first_f + i * SCAN_UNROLL + j
                tr = first_r - i * SCAN_UNROLL - j
                of, hf = _tile_scan(a_ref[0, tf], b_ref[0, tf], hf, row, False)
                orv, hr = _tile_scan(a_ref[1, tr], b_ref[1, tr], hr, row, True)
                hs_ref[0, tf] = of
                hs_ref[1, tr] = orv
            return hf, hr
        return lax.fori_loop(0, n // SCAN_UNROLL, body, (hf, hr))

    zero = jnp.zeros((SUBLANE, LRU_WIDTH), F32)
    hf, hr = run(0, nc - 1, nc, zero, zero)
    run(nc, nc + nl - 1, nl, hf, hr)

    grec = grec_ref[...]
    for c in range(ttot // GATE_ROWS):
        sl = pl.ds(c * tiles_per_chunk, tiles_per_chunk)
        hsum = (hs_ref[0, sl] + hs_ref[1, sl]).reshape(GATE_ROWS, LRU_WIDTH)
        r0 = c * GATE_ROWS
        if r0 < tc:
            gate = grc_ref[0, pl.ds(r0, GATE_ROWS), :]
        else:
            gate = grl_ref[0, pl.ds(r0 - tc, GATE_ROWS), :]
        y = _rms(hsum * jax.nn.gelu(gate), grec).astype(BF16)
        if r0 < tc:
            outc_ref[0, pl.ds(r0, GATE_ROWS), :] = y
        else:
            outl_ref[0, pl.ds(r0 - tc, GATE_ROWS), :] = y


def _lru_call(xr_c, xr_l, gr_c, gr_l, lw, l):
    B, tc, _ = xr_c.shape
    tl = xr_l.shape[1]
    ttot = tc + tl
    assert tc % GATE_ROWS == 0 and tl % GATE_ROWS == 0
    seq = lambda n: pl.BlockSpec((1, n, LRU_WIDTH), lambda b: (b, 0, 0))
    W = functools.partial(_layer_spec, l=l)
    return pl.pallas_call(
        functools.partial(_lru_kernel, tc=tc, tl=tl),
        out_shape=[jax.ShapeDtypeStruct((B, tc, LRU_WIDTH), BF16),
                   jax.ShapeDtypeStruct((B, tl, LRU_WIDTH), BF16)],
        grid=(B,),
        in_specs=[seq(tc), seq(tl), seq(tc), seq(tl),
                  W((CONV_W, LRU_WIDTH)), W((1, LRU_WIDTH)),
                  W((LRU_WIDTH, 4 * LRU_WIDTH)), W((1, 4 * LRU_WIDTH)),
                  W((2, LRU_WIDTH)), W((1, LRU_WIDTH))],
        out_specs=[seq(tc), seq(tl)],
        scratch_shapes=[
            pltpu.VMEM((tl + 2 * PAD_ROWS, LRU_WIDTH), F32),
            pltpu.VMEM((ttot, LRU_WIDTH), F32),
            pltpu.VMEM((2, ttot // SUBLANE, SUBLANE, LRU_WIDTH), F32),
            pltpu.VMEM((2, ttot // SUBLANE, SUBLANE, LRU_WIDTH), F32),
            pltpu.VMEM((2, ttot // SUBLANE, SUBLANE, LRU_WIDTH), F32),
        ],
        compiler_params=pltpu.CompilerParams(
            dimension_semantics=("parallel",), vmem_limit_bytes=VMEM_LIMIT),
        name="lru",
    )(xr_c, xr_l, gr_c, gr_l, lw["conv_w"], lw["conv_b"], lw["w_gates"], lw["b_gates"],
      lw["lru_lam"], lw["g_rec"])


RED_ROWS = 64


def _col_reduce(x, op):
    rows, cols = x.shape
    part = op(x.reshape(rows // RED_ROWS, RED_ROWS, cols), axis=0)
    return op(part, axis=0, keepdims=True)


def _attn_kernel(*refs, nseg):
    q_ref = refs[0]
    kv_refs = refs[1:1 + 2 * nseg]
    g_ref = refs[1 + 2 * nseg]
    o_ref = refs[2 + 2 * nseg]
    s_scr = refs[3 + 2 * nseg:5 + 2 * nseg]
    o_scr = refs[5 + 2 * nseg]
    seg_len = [kv_refs[2 * s].shape[2] for s in range(nseg)]
    seg_off = [sum(seg_len[:s]) for s in range(nseg)]

    def scores_into(hd, slot):
        q = q_ref[0, hd]
        m = None
        for s in range(nseg):
            sc = lax.dot_general(kv_refs[2 * s][0, hd], q, (((1,), (1,)), ((), ())),
                                 preferred_element_type=F32)
            s_scr[slot][pl.ds(seg_off[s], seg_len[s]), :] = sc
            ms = _col_reduce(sc, jnp.max)
            m = ms if m is None else jnp.maximum(m, ms)
        return m

    def softmax_pv(hd, slot, m):
        o = None
        for s in range(nseg):
            rows = pl.ds(seg_off[s], seg_len[s])
            p = jnp.exp2(s_scr[slot][rows, :] - m).astype(BF16)
            po = _dot(kv_refs[2 * s + 1][0, hd], p)
            o = po if o is None else o + po
        o_scr[hd] = o[:V_DIM] * (1.0 / o[V_DIM:V_DIM + 1])

    def trip(i, m0):
        h0 = 2 * i
        m1 = scores_into(h0 + 1, 1)
        softmax_pv(h0, 0, m0)
        m2 = scores_into(h0 + 2, 0)
        softmax_pv(h0 + 1, 1, m1)
        return m2

    m6 = lax.fori_loop(0, HEADS // 2 - 1, trip, scores_into(0, 0))
    m7 = scores_into(HEADS - 1, 1)
    softmax_pv(HEADS - 2, 0, m6)
    softmax_pv(HEADS - 1, 1, m7)
    att = o_scr[...].reshape(MLA_WIDTH, o_scr.shape[2]).T
    o_ref[0] = _rms(att, g_ref[...]).astype(BF16)


def _attn_call(q, segs, g_att, l, tq):
    B, _, S, _ = q.shape
    nseg = len(segs)
    ttot = sum(k.shape[2] for k, _ in segs)
    in_specs = [pl.BlockSpec((1, HEADS, tq, HEAD_PAD), lambda b, t: (b, 0, t, 0))]
    args = [q]
    for k, v in segs:
        tn = k.shape[2]
        in_specs.append(pl.BlockSpec((1, HEADS, tn, HEAD_PAD), lambda b, t: (b, 0, 0, 0)))
        in_specs.append(pl.BlockSpec((1, HEADS, V_EXT, tn), lambda b, t: (b, 0, 0, 0)))
        args += [k, v]
    in_specs.append(_layer_spec((1, MLA_WIDTH), l))
    args.append(g_att)
    return pl.pallas_call(
        functools.partial(_attn_kernel, nseg=nseg),
        out_shape=jax.ShapeDtypeStruct((B, S, MLA_WIDTH), BF16),
        grid=(B, S // tq),
        in_specs=in_specs,
        out_specs=pl.BlockSpec((1, tq, MLA_WIDTH), lambda b, t: (b, t, 0)),
        scratch_shapes=[pltpu.VMEM((ttot, tq), F32), pltpu.VMEM((ttot, tq), F32),
                        pltpu.VMEM((HEADS, V_DIM, tq), F32)],
        compiler_params=pltpu.CompilerParams(
            dimension_semantics=("parallel", "arbitrary"), vmem_limit_bytes=VMEM_LIMIT),
        name="attn",
    )(*args)


def _out_kernel(x_ref, att_ref, rec_ref, sp_ref, mod_ref, n2_ref, wo_ref, wfi_ref, wfo_ref,
                fn_ref, o_ref, *, final):
    mod = mod_ref[...]
    y = _dot(att_ref[0], wo_ref[pl.ds(0, MLA_WIDTH), :])
    y = y + _dot(rec_ref[0], wo_ref[pl.ds(MLA_WIDTH, LRU_WIDTH), :])
    y = y + _dot(sp_ref[0], wo_ref[pl.ds(MLA_WIDTH + LRU_WIDTH, SGU_WIDTH), :])
    x1 = x_ref[0] + mod[2:3] * y
    h2 = (_rms(x1, n2_ref[...]) * (1.0 + mod[4:5]) + mod[3:4]).astype(BF16)
    acc = None
    for c0, cw in FF_CHUNKS:
        g = _dot(h2, wfi_ref[:, pl.ds(c0, cw)])
        u = _dot(h2, wfi_ref[:, pl.ds(D_FF + c0, cw)])
        act = (jax.nn.silu(g) * u).astype(BF16)
        part = _dot(act, wfo_ref[pl.ds(c0, cw), :])
        acc = part if acc is None else acc + part
    x2 = x1 + mod[5:6] * acc
    if final:
        x2 = _rms(x2, fn_ref[...])
    o_ref[0] = x2


def _out_call(x, att, rec, sp, mod, mod_row, lw, l, final_norm, final, tm):
    B, T, _ = x.shape
    tok = lambda w: pl.BlockSpec((1, tm, w), lambda b, t: (b, t, 0))
    W = functools.partial(_layer_spec, l=l)
    return pl.pallas_call(
        functools.partial(_out_kernel, final=final),
        out_shape=jax.ShapeDtypeStruct((B, T, D_MODEL), F32),
        grid=(B, T // tm),
        in_specs=[tok(D_MODEL), tok(MLA_WIDTH), tok(LRU_WIDTH), tok(SGU_WIDTH),
                  _mod_spec(l, mod_row),
                  W((1, D_MODEL)),
                  W((D_MODEL, D_MODEL)),
                  W((D_MODEL, 2 * D_FF)),
                  W((D_FF, D_MODEL)),
                  _const_spec((1, D_MODEL))],
        out_specs=tok(D_MODEL),
        compiler_params=pltpu.CompilerParams(
            dimension_semantics=("parallel", "parallel"), vmem_limit_bytes=VMEM_LIMIT),
        name="out_ffn",
    )(x, att, rec, sp, mod, lw["norm2"], lw["w_out"], lw["w_ffn_in"], lw["w_ffn_out"], final_norm)


def _rot_cols(w):
    q = ROPE // 4
    return jnp.concatenate([-w[..., q:2 * q], w[..., 0:q], -w[..., 3 * q:4 * q], w[..., 2 * q:3 * q]], axis=-1)


def _prep_params(p):
    L = DEPTH
    splits = np.cumsum([Q_LORA, KV_LORA, ROPE, LRU_WIDTH, LRU_WIDTH, SGU_WIDTH])
    qa, kva, kr, xr, gr, su, sv = jnp.split(p["w_in"], splits, axis=2)
    kr_blk = jnp.concatenate([kr, _rot_cols(kr), jnp.zeros((L, D_MODEL, LANE - 2 * ROPE), F32)], axis=2)
    w_in_ext = jnp.concatenate([qa, kva, kr_blk, xr, gr, su, sv], axis=2).astype(BF16)

    wq = p["w_q_b"].reshape(L, Q_LORA, HEADS, NOPE + ROPE)
    zpad = jnp.zeros((L, Q_LORA, HEADS, HEAD_PAD - NOPE - ROPE), F32)
    w_q = jnp.concatenate([wq, zpad], axis=-1).reshape(L, Q_LORA, HEADS * HEAD_PAD).astype(BF16)

    wkv = p["w_kv_b"].reshape(L, KV_LORA, HEADS, NOPE + V_DIM)
    wk = jnp.concatenate([wkv[..., :NOPE], jnp.zeros((L, KV_LORA, HEADS, HEAD_PAD - NOPE), F32)],
                         axis=-1).reshape(L, KV_LORA, HEADS * HEAD_PAD)
    wv = wkv[..., NOPE:].reshape(L, KV_LORA, MLA_WIDTH)
    place = np.zeros((LANE, HEADS, HEAD_PAD), np.float32)
    place[np.arange(ROPE), :, NOPE + np.arange(ROPE)] = 1.0
    place = jnp.broadcast_to(jnp.asarray(place.reshape(LANE, HEADS * HEAD_PAD)), (L, LANE, HEADS * HEAD_PAD))
    w_kv = jnp.concatenate([wk, place], axis=1).astype(BF16)
    w_vt = jnp.swapaxes(wv, 1, 2).astype(BF16)

    eye = np.eye(LRU_HEADS, dtype=np.float32)[None, :, None, :, None]

    def block_diag(w):
        _, hh, bi, bj = w.shape
        return (eye * w[:, :, :, None, :]).reshape(L, hh * bi, hh * bj)

    w_gates = jnp.concatenate([block_diag(p["lru_w_r"][:, 0]), block_diag(p["lru_w_i"][:, 0]),
                               block_diag(p["lru_w_r"][:, 1]), block_diag(p["lru_w_i"][:, 1])],
                              axis=2).astype(BF16)
    b_gates = jnp.concatenate([p["lru_b_r"][:, 0], p["lru_b_i"][:, 0],
                               p["lru_b_r"][:, 1], p["lru_b_i"][:, 1]], axis=1)[:, None]

    sgu_wcat = jnp.transpose(p["sgu_w"], (0, 2, 1, 3)).reshape(L, CHUNK, SGU_GROUPS * CHUNK).astype(BF16)
    sgu_bias = jnp.repeat(jnp.swapaxes(p["sgu_b"], 1, 2), SGU_GROUP_DIM, axis=2)
    gidx = np.arange(SGU_WIDTH) // SGU_GROUP_DIM
    gmat = jnp.asarray((gidx[:, None] == gidx[None, :]).astype(np.float32) / SGU_GROUP_DIM, dtype=BF16)

    on = p["out_norm"][:, None]
    row = lambda a: a[:, None]
    return {
        "norm1": row(p["norm1"]), "norm2": row(p["norm2"]),
        "w_in": w_in_ext, "q_a_norm": row(p["q_a_norm"]), "w_q": w_q,
        "kv_a_norm": row(p["kv_a_norm"]), "w_kv": w_kv, "w_vt": w_vt,
        "conv_w": p["conv_w"], "conv_b": row(p["conv_b"]),
        "w_gates": w_gates, "b_gates": b_gates, "lru_lam": p["lru_lam"],
        "sgu_wcat": sgu_wcat, "sgu_bias": sgu_bias, "sgu_norm": row(p["sgu_norm"]), "gmat": gmat,
        "g_att": on[..., :MLA_WIDTH], "g_rec": on[..., MLA_WIDTH:MLA_WIDTH + LRU_WIDTH],
        "g_sp": on[..., MLA_WIDTH + LRU_WIDTH:],
        "w_out": p["w_out"].astype(BF16), "w_ffn_in": p["w_ffn_in"].astype(BF16),
        "w_ffn_out": p["w_ffn_out"].astype(BF16),
    }


def _rope_tables(seq_len, ctx_len):
    f32 = np.float32
    rows = seq_len // GRID_W
    row = np.repeat(np.arange(rows, dtype=f32), GRID_W)
    col = np.tile(np.arange(GRID_W, dtype=f32), rows)
    half = ROPE // 2
    freq = (f32(ROPE_BASE) ** (-np.arange(0, half, 2, dtype=f32) / f32(half))).astype(f32)
    ar = row[:, None] * freq
    ac = col[:, None] * freq
    ang = np.concatenate([ar, ar, ac, ac], axis=-1).astype(f32)
    cos, sin = np.cos(ang).astype(f32), np.sin(ang).astype(f32)
    qs = f32(math.log2(math.e) / math.sqrt(NOPE + ROPE))

    up = ((np.arange(ROPE) // ROT_SHIFT) % 2 == 0).astype(f32)

    def q_tabs(c, s, n):
        pad = np.zeros((n, HEAD_PAD - NOPE - ROPE), f32)
        nope0 = np.zeros((n, NOPE), f32)
        cq = np.concatenate([np.ones((n, NOPE), f32), c, pad], axis=1)
        sa = np.concatenate([nope0, -s * up, pad], axis=1)
        sb = np.concatenate([nope0, s * (1 - up), pad], axis=1)
        tk = np.concatenate([c, s, np.zeros((n, LANE - 2 * ROPE), f32)], axis=1)
        return jnp.asarray(cq * qs), jnp.asarray(sa * qs), jnp.asarray(sb * qs), jnp.asarray(tk)

    lat = q_tabs(cos, sin, seq_len)
    ctx = q_tabs(np.ones((ctx_len, ROPE), f32), np.zeros((ctx_len, ROPE), f32), ctx_len)
    return lat, ctx


def kernel(x, c, ctx, c_ctx, norm1, norm2, w_ada, b_ada, w_in, q_a_norm, w_q_b, kv_a_norm, w_kv_b, conv_w, conv_b, lru_w_r, lru_b_r, lru_w_i, lru_b_i, lru_lam, sgu_norm, sgu_w, sgu_b, out_norm, w_out, w_ffn_in, w_ffn_out, final_norm):
    p = dict(norm1=norm1, norm2=norm2, w_in=w_in, q_a_norm=q_a_norm, w_q_b=w_q_b, kv_a_norm=kv_a_norm,
             w_kv_b=w_kv_b, conv_w=conv_w, conv_b=conv_b, lru_w_r=lru_w_r, lru_b_r=lru_b_r,
             lru_w_i=lru_w_i, lru_b_i=lru_b_i, lru_lam=lru_lam, sgu_norm=sgu_norm, sgu_w=sgu_w,
             sgu_b=sgu_b, out_norm=out_norm, w_out=w_out, w_ffn_in=w_ffn_in, w_ffn_out=w_ffn_out)
    B, S, _ = x.shape
    tctx = ctx.shape[1]
    mod_rows = ((B + 1 + SUBLANE - 1) // SUBLANE) * SUBLANE
    cc = jnp.concatenate([c, c_ctx[None], jnp.zeros((mod_rows - B - 1, D_MODEL), F32)], axis=0)
    mod = _ada_call(cc, w_ada, b_ada).reshape(DEPTH, mod_rows, 6, D_MODEL)
    tabs_l, tabs_c = _rope_tables(S, tctx)
    fn = final_norm[None]
    lw = _prep_params(p)

    h_ctx = ctx
    for l in range(DEPTH):
        last = l == DEPTH - 1
        q_l, k_l, v_l, xr_l, gr_l, sp_l = _inproj_call(x, mod, None, tabs_l, lw, l, 512)
        q_c, k_c, v_c, xr_c, gr_c, sp_c = _inproj_call(h_ctx, mod, B, tabs_c, lw, l, 256)
        rec_c, rec_l = _lru_call(xr_c, xr_l, gr_c, gr_l, lw, l)
        att_l = _attn_call(q_l, [(k_c, v_c), (k_l, v_l)], lw["g_att"], l, 256)
        x = _out_call(x, att_l, rec_l, sp_l, mod, None, lw, l, fn, last, 512)
        if not last:
            att_c = _attn_call(q_c, [(k_c, v_c)], lw["g_att"], l, 256)
            h_ctx = _out_call(h_ctx, att_c, rec_c, sp_c, mod, B, lw, l, fn, False, 256)
    return x
```

```python
import functools
import math

import jax
import jax.numpy as jnp
import numpy as np
from jax import lax
from jax.experimental import pallas as pl
from jax.experimental.pallas import tpu as pltpu

F32 = jnp.float32
BF16 = jnp.bfloat16

D_MODEL = 1024
DEPTH = 2
GRID_W = 64
EPS = 1e-6
ROPE_BASE = 10000.0
HEADS = 8
NOPE = 64
ROPE = 32
V_DIM = 64
Q_LORA = 256
KV_LORA = 128
LRU_WIDTH = 256
LRU_HEADS = 4
CONV_W = 4
LRU_C = 8.0
SGU_GROUPS = 4
SGU_WIDTH = 256
SGU_GROUP_DIM = SGU_WIDTH // SGU_GROUPS
CHUNK = 128
D_FF = 2816
MLA_WIDTH = HEADS * V_DIM
V_EXT = V_DIM + 16

LANE = 128
SUBLANE = 8
HEAD_PAD = LANE
ROT_SHIFT = ROPE // 4
VMEM_LIMIT = 56 * 1024 * 1024

IN_EXT = 1536
C_QA, C_KVA, C_KR, C_XR, C_GR, C_SU, C_SV = 0, 256, 384, 512, 768, 1024, 1280
MXU_TILE = 256
FF_CHUNKS = ((0, 6 * MXU_TILE), (6 * MXU_TILE, 5 * MXU_TILE))


def _rms(x, g):
    ms = jnp.mean(x * x, axis=-1, keepdims=True)
    return x * lax.rsqrt(ms + EPS) * g


def _dot(a, b):
    return jnp.dot(a, b, preferred_element_type=F32)


def _const_spec(shape):
    zeros = (0,) * len(shape)
    return pl.BlockSpec(shape, lambda *_: zeros, pipeline_mode=pl.Buffered(1))


def _layer_spec(shape, l):
    zeros = (0,) * len(shape)
    return pl.BlockSpec((None,) + tuple(shape), lambda *_: (l,) + zeros, pipeline_mode=pl.Buffered(1))


def _mod_spec(l, row):
    if row is None:
        return pl.BlockSpec((None, None, 6, D_MODEL), lambda b, t: (l, b, 0, 0))
    return pl.BlockSpec((None, None, 6, D_MODEL), lambda b, t: (l, row, 0, 0))


ADA_TILE = 512


def _ada_kernel(c_ref, w_ref, b_ref, o_ref):
    s = jax.nn.silu(c_ref[...]).astype(BF16)
    o_ref[0] = _dot(s, w_ref[0].astype(BF16)) + b_ref[0]


def _ada_call(cc, w_ada, b_ada):
    rows = cc.shape[0]
    n = w_ada.shape[-1]
    return pl.pallas_call(
        _ada_kernel,
        out_shape=jax.ShapeDtypeStruct((DEPTH, rows, n), F32),
        grid=(DEPTH, n // ADA_TILE),
        in_specs=[
            pl.BlockSpec((rows, D_MODEL), lambda l, j: (0, 0)),
            pl.BlockSpec((1, D_MODEL, ADA_TILE), lambda l, j: (l, 0, j)),
            pl.BlockSpec((1, 1, ADA_TILE), lambda l, j: (l, 0, j)),
        ],
        out_specs=pl.BlockSpec((1, rows, ADA_TILE), lambda l, j: (l, 0, j)),
        compiler_params=pltpu.CompilerParams(
            dimension_semantics=("arbitrary", "arbitrary"), vmem_limit_bytes=VMEM_LIMIT),
        name="ada_mod",
    )(cc, w_ada, b_ada.reshape(DEPTH, 1, n))


def _inproj_kernel(x_ref, mod_ref, n1_ref, win_ref, qan_ref, wq_ref, kvn_ref, wkv_ref, wvt_ref,
                   cq_ref, sa_ref, sb_ref, tk_ref, wcat_ref, sbias_ref, sgn_ref, gmat_ref, gsp_ref,
                   q_ref, k_ref, v_ref, xr_ref, gr_ref, sp_ref, *, tm):
    x = x_ref[0]
    mod = mod_ref[...]
    h = _rms(x, n1_ref[...]) * (1.0 + mod[1:2]) + mod[0:1]
    z = _dot(h.astype(BF16), win_ref[...])

    qn = _rms(z[:, C_QA:C_QA + Q_LORA], qan_ref[...]).astype(BF16)
    qq = _dot(qn, wq_ref[...])
    cq = cq_ref[...]
    sa = sa_ref[...]
    sb = sb_ref[...]
    for hd in range(HEADS):
        qh = qq[:, hd * HEAD_PAD:(hd + 1) * HEAD_PAD]
        qh = (qh * cq + pltpu.roll(qh, LANE - ROT_SHIFT, axis=1) * sa
              + pltpu.roll(qh, ROT_SHIFT, axis=1) * sb)
        q_ref[0, hd] = qh.astype(BF16)

    kvn = _rms(z[:, C_KVA:C_KVA + KV_LORA], kvn_ref[...]).astype(BF16)
    t = z[:, C_KR:C_KR + LANE] * tk_ref[...]
    kro = t + pltpu.roll(t, LANE - ROPE, axis=1)
    lhs = jnp.concatenate([kvn, kro.astype(BF16)], axis=1)
    kk = _dot(lhs, wkv_ref[...])
    for hd in range(HEADS):
        k_ref[0, hd] = kk[:, hd * HEAD_PAD:(hd + 1) * HEAD_PAD].astype(BF16)
    vt = lax.dot_general(wvt_ref[...], kvn, (((1,), (1,)), ((), ())), preferred_element_type=F32)
    tail_row = lax.broadcasted_iota(jnp.int32, (V_EXT - V_DIM, tm), 0)
    tail = jnp.where(tail_row == 0, 1.0, 0.0).astype(BF16)
    for hd in range(HEADS):
        v_ref[0, hd] = jnp.concatenate([vt[hd * V_DIM:(hd + 1) * V_DIM].astype(BF16), tail], axis=0)

    xr_ref[0] = z[:, C_XR:C_XR + LRU_WIDTH]
    gr_ref[0] = z[:, C_GR:C_GR + LRU_WIDTH]

    u = jax.nn.gelu(z[:, C_SU:C_SU + SGU_WIDTH])
    vg = jax.nn.gelu(z[:, C_SV:C_SV + SGU_WIDTH])
    v2 = vg * vg
    v2_hi = v2.astype(BF16)
    v2_lo = (v2 - v2_hi.astype(F32)).astype(BF16)
    gmat = gmat_ref[...]
    gms = _dot(v2_hi, gmat) + _dot(v2_lo, gmat)
    vb = (vg * lax.rsqrt(gms + EPS) * sgn_ref[...]).astype(BF16)
    grp = lax.broadcasted_iota(jnp.int32, (CHUNK, SGU_WIDTH), 1) // SGU_GROUP_DIM
    zero = jnp.zeros((CHUNK, SGU_WIDTH), BF16)
    wcat = wcat_ref[...]
    sbias = sbias_ref[...]
    parts = []
    for c in range(tm // CHUNK):
        vc = vb[c * CHUNK:(c + 1) * CHUNK]
        rhs = jnp.concatenate([jnp.where(grp == g, vc, zero) for g in range(SGU_GROUPS)], axis=0)
        s = _dot(wcat, rhs) + sbias
        parts.append(u[c * CHUNK:(c + 1) * CHUNK] * s)
    sp = jnp.concatenate(parts, axis=0) if len(parts) > 1 else parts[0]
    sp_ref[0] = _rms(sp, gsp_ref[...]).astype(BF16)


def _inproj_call(x, mod, mod_row, tabs, lw, l, tm):
    B, T, _ = x.shape
    cq, sa, sb, tk = tabs
    tab_spec = pl.BlockSpec((tm, LANE), lambda b, t: (t, 0))
    W = functools.partial(_layer_spec, l=l)
    in_specs = [
        pl.BlockSpec((1, tm, D_MODEL), lambda b, t: (b, t, 0)),
        _mod_spec(l, mod_row),
        W((1, D_MODEL)),
        W((D_MODEL, IN_EXT)),
        W((1, Q_LORA)),
        W((Q_LORA, HEADS * HEAD_PAD)),
        W((1, KV_LORA)),
        W((2 * LANE, HEADS * HEAD_PAD)),
        W((MLA_WIDTH, KV_LORA)),
        tab_spec, tab_spec, tab_spec, tab_spec,
        W((CHUNK, SGU_GROUPS * CHUNK)),
        W((CHUNK, SGU_WIDTH)),
        W((1, SGU_WIDTH)),
        _const_spec((SGU_WIDTH, SGU_WIDTH)),
        W((1, SGU_WIDTH)),
    ]
    out_shape = [
        jax.ShapeDtypeStruct((B, HEADS, T, HEAD_PAD), BF16),
        jax.ShapeDtypeStruct((B, HEADS, T, HEAD_PAD), BF16),
        jax.ShapeDtypeStruct((B, HEADS, V_EXT, T), BF16),
        jax.ShapeDtypeStruct((B, T, LRU_WIDTH), F32),
        jax.ShapeDtypeStruct((B, T, LRU_WIDTH), F32),
        jax.ShapeDtypeStruct((B, T, SGU_WIDTH), BF16),
    ]
    out_specs = [
        pl.BlockSpec((1, HEADS, tm, HEAD_PAD), lambda b, t: (b, 0, t, 0)),
        pl.BlockSpec((1, HEADS, tm, HEAD_PAD), lambda b, t: (b, 0, t, 0)),
        pl.BlockSpec((1, HEADS, V_EXT, tm), lambda b, t: (b, 0, 0, t)),
        pl.BlockSpec((1, tm, LRU_WIDTH), lambda b, t: (b, t, 0)),
        pl.BlockSpec((1, tm, LRU_WIDTH), lambda b, t: (b, t, 0)),
        pl.BlockSpec((1, tm, SGU_WIDTH), lambda b, t: (b, t, 0)),
    ]
    return pl.pallas_call(
        functools.partial(_inproj_kernel, tm=tm),
        out_shape=out_shape,
        grid=(B, T // tm),
        in_specs=in_specs,
        out_specs=out_specs,
        compiler_params=pltpu.CompilerParams(
            dimension_semantics=("parallel", "parallel"), vmem_limit_bytes=VMEM_LIMIT),
        name="inproj",
    )(x, mod, lw["norm1"], lw["w_in"], lw["q_a_norm"], lw["w_q"], lw["kv_a_norm"], lw["w_kv"], lw["w_vt"],
      cq, sa, sb, tk, lw["sgu_wcat"], lw["sgu_bias"], lw["sgu_norm"], lw["gmat"], lw["g_sp"])


GATE_ROWS = 256
PAD_ROWS = SUBLANE
SCAN_UNROLL = 4


def _sigmoid(x):
    return 0.5 * jnp.tanh(0.5 * x) + 0.5


def _tile_scan(a, b, h_prev, row, reverse):
    for k in (1, 2, 4):
        if reverse:
            keep = row < SUBLANE - k
            sh = SUBLANE - k
        else:
            keep = row >= k
            sh = k
        a_s = jnp.where(keep, pltpu.roll(a, sh, axis=0), 1.0)
        b_s = jnp.where(keep, pltpu.roll(b, sh, axis=0), 0.0)
        b = a * b_s + b
        a = a * a_s
    h = a * h_prev + b
    last = h[0:1] if reverse else h[SUBLANE - 1:SUBLANE]
    return h, jnp.broadcast_to(last, h.shape)


def _lru_kernel(xrc_ref, xrl_ref, grc_ref, grl_ref, cw_ref, cb_ref, wg_ref, bg_ref, lam_ref,
                grec_ref, outc_ref, outl_ref, pad_ref, xc_ref, a_ref, b_ref, hs_ref, *, tc, tl):
    cw = cw_ref[...]
    cb = cb_ref[...]
    ttot = tc + tl

    def conv(src_ref, n, dst0):
        pad_ref[pl.ds(0, PAD_ROWS), :] = jnp.zeros((PAD_ROWS, LRU_WIDTH), F32)
        pad_ref[pl.ds(PAD_ROWS, n), :] = src_ref[0]
        pad_ref[pl.ds(PAD_ROWS + n, PAD_ROWS), :] = jnp.zeros((PAD_ROWS, LRU_WIDTH), F32)
        y = cb
        for j in range(CONV_W):
            y = y + cw[j:j + 1] * pad_ref[pl.ds(PAD_ROWS - CONV_W // 2 + j, n), :]
        xc_ref[pl.ds(dst0, n), :] = y

    conv(xrc_ref, tc, 0)
    conv(xrl_ref, tl, tc)

    nsp = -LRU_C * jax.nn.softplus(-lam_ref[...])
    wg = wg_ref[...]
    bg = bg_ref[...]
    tiles_per_chunk = GATE_ROWS // SUBLANE
    for c in range(ttot // GATE_ROWS):
        xcc = xc_ref[pl.ds(c * GATE_ROWS, GATE_ROWS), :]
        g = _dot(xcc.astype(BF16), wg) + bg
        for d in range(2):
            r = _sigmoid(g[:, (2 * d) * LRU_WIDTH:(2 * d + 1) * LRU_WIDTH])
            i = _sigmoid(g[:, (2 * d + 1) * LRU_WIDTH:(2 * d + 2) * LRU_WIDTH])
            log_a = nsp[d:d + 1] * r
            a = jnp.exp(log_a)
            bb = jnp.sqrt(-jnp.tanh(log_a) * (a * a + 1.0)) * (i * xcc)
            a_ref[d, pl.ds(c * tiles_per_chunk, tiles_per_chunk)] = a.reshape(
                tiles_per_chunk, SUBLANE, LRU_WIDTH)
            b_ref[d, pl.ds(c * tiles_per_chunk, tiles_per_chunk)] = bb.reshape(
                tiles_per_chunk, SUBLANE, LRU_WIDTH)

    row = lax.broadcasted_iota(jnp.int32, (SUBLANE, LRU_WIDTH), 0)
    nc = tc // SUBLANE
    nl = tl // SUBLANE

    def run(first_f, first_r, n, hf, hr):
        def body(i, carry):
            hf, hr = carry
            for j in range(SCAN_UNROLL):
                tf = first_f + i * SCAN_UNROLL + j
                tr = first_r - i * SCAN_UNROLL - j
                of, hf = _tile_scan(a_ref[0, tf], b_ref[0, tf], hf, row, False)
                orv, hr = _tile_scan(a_ref[1, tr], b_ref[1, tr], hr, row, True)
                hs_ref[0, tf] = of
                hs_ref[1, tr] = orv
            return hf, hr
        return lax.fori_loop(0, n // SCAN_UNROLL, body, (hf, hr))

    zero = jnp.zeros((SUBLANE, LRU_WIDTH), F32)
    hf, hr = run(0, nc - 1, nc, zero, zero)
    run(nc, nc + nl - 1, nl, hf, hr)

    grec = grec_ref[...]
    for c in range(ttot // GATE_ROWS):
        sl = pl.ds(c * tiles_per_chunk, tiles_per_chunk)
        hsum = (hs_ref[0, sl] + hs_ref[1, sl]).reshape(GATE_ROWS, LRU_WIDTH)
        r0 = c * GATE_ROWS
        if r0 < tc:
            gate = grc_ref[0, pl.ds(r0, GATE_ROWS), :]
        else:
            gate = grl_ref[0, pl.ds(r0 - tc, GATE_ROWS), :]
        y = _rms(hsum * jax.nn.gelu(gate), grec).astype(BF16)
        if r0 < tc:
            outc_ref[0, pl.ds(r0, GATE_ROWS), :] = y
        else:
            outl_ref[0, pl.ds(r0 - tc, GATE_ROWS), :] = y


def _lru_call(xr_c, xr_l, gr_c, gr_l, lw, l):
    B, tc, _ = xr_c.shape
    tl = xr_l.shape[1]
    ttot = tc + tl
    assert tc % GATE_ROWS == 0 and tl % GATE_ROWS == 0
    seq = lambda n: pl.BlockSpec((1, n, LRU_WIDTH), lambda b: (b, 0, 0))
    W = functools.partial(_layer_spec, l=l)
    return pl.pallas_call(
        functools.partial(_lru_kernel, tc=tc, tl=tl),
        out_shape=[jax.ShapeDtypeStruct((B, tc, LRU_WIDTH), BF16),
                   jax.ShapeDtypeStruct((B, tl, LRU_WIDTH), BF16)],
        grid=(B,),
        in_specs=[seq(tc), seq(tl), seq(tc), seq(tl),
                  W((CONV_W, LRU_WIDTH)), W((1, LRU_WIDTH)),
                  W((LRU_WIDTH, 4 * LRU_WIDTH)), W((1, 4 * LRU_WIDTH)),
                  W((2, LRU_WIDTH)), W((1, LRU_WIDTH))],
        out_specs=[seq(tc), seq(tl)],
        scratch_shapes=[
            pltpu.VMEM((tl + 2 * PAD_ROWS, LRU_WIDTH), F32),
            pltpu.VMEM((ttot, LRU_WIDTH), F32),
            pltpu.VMEM((2, ttot // SUBLANE, SUBLANE, LRU_WIDTH), F32),
            pltpu.VMEM((2, ttot // SUBLANE, SUBLANE, LRU_WIDTH), F32),
            pltpu.VMEM((2, ttot // SUBLANE, SUBLANE, LRU_WIDTH), F32),
        ],
        compiler_params=pltpu.CompilerParams(
            dimension_semantics=("parallel",), vmem_limit_bytes=VMEM_LIMIT),
        name="lru",
    )(xr_c, xr_l, gr_c, gr_l, lw["conv_w"], lw["conv_b"], lw["w_gates"], lw["b_gates"],
      lw["lru_lam"], lw["g_rec"])


RED_ROWS = 64
ATTN_UNROLL = 8


def _col_reduce(x, op):
    rows, cols = x.shape
    part = op(x.reshape(rows // RED_ROWS, RED_ROWS, cols), axis=0)
    return op(part, axis=0, keepdims=True)


def _attn_kernel(*refs, nseg, tq):
    q_ref = refs[0]
    kv_refs = refs[1:1 + 2 * nseg]
    g_ref = refs[1 + 2 * nseg]
    o_ref = refs[2 + 2 * nseg]
    s_scr = refs[3 + 2 * nseg:5 + 2 * nseg]
    o_scr = refs[5 + 2 * nseg]
    seg_len = [kv_refs[2 * s].shape[2] for s in range(nseg)]
    seg_off = [sum(seg_len[:s]) for s in range(nseg)]

    nsub = q_ref.shape[2] // tq

    def scores_into(sub, hd, slot):
        q = q_ref[0, hd, pl.ds(pl.multiple_of(sub * tq, tq), tq), :]
        m = None
        for s in range(nseg):
            sc = lax.dot_general(kv_refs[2 * s][0, hd], q, (((1,), (1,)), ((), ())),
                                 preferred_element_type=F32)
            s_scr[slot][pl.ds(seg_off[s], seg_len[s]), :] = sc
            ms = _col_reduce(sc, jnp.max)
            m = ms if m is None else jnp.maximum(m, ms)
        return m

    def softmax_pv(sub, hd, slot, m):
        o = None
        for s in range(nseg):
            rows = pl.ds(seg_off[s], seg_len[s])
            p = jnp.exp2(s_scr[slot][rows, :] - m).astype(BF16)
            po = _dot(kv_refs[2 * s + 1][0, hd], p)
            o = po if o is None else o + po
        o_scr[sub, hd] = o[:V_DIM] * (1.0 / o[V_DIM:V_DIM + 1])

    def run_units(u0, n_units, m, has_next):
        for j in range(n_units):
            u = u0 + j
            if j + 1 < n_units or has_next:
                m_next = scores_into((u + 1) // HEADS, (u + 1) % HEADS, (j + 1) % 2)
            else:
                m_next = None
            softmax_pv(u // HEADS, u % HEADS, j % 2, m)
            m = m_next
        return m

    n_units = nsub * HEADS
    m_a = lax.fori_loop(0, n_units // ATTN_UNROLL - 1,
                        lambda i, m: run_units(i * ATTN_UNROLL, ATTN_UNROLL, m, True),
                        scores_into(0, 0, 0))
    run_units(n_units - ATTN_UNROLL, ATTN_UNROLL, m_a, False)
    for sub in range(nsub):
        att = o_scr[sub].reshape(MLA_WIDTH, tq).T
        o_ref[0, pl.ds(sub * tq, tq), :] = _rms(att, g_ref[...]).astype(BF16)


def _attn_call(q, segs, g_att, l, tq_blk, tq):
    B, _, S, _ = q.shape
    nseg = len(segs)
    ttot = sum(k.shape[2] for k, _ in segs)
    in_specs = [pl.BlockSpec((1, HEADS, tq_blk, HEAD_PAD), lambda b, t: (b, 0, t, 0))]
    args = [q]
    for k, v in segs:
        tn = k.shape[2]
        in_specs.append(pl.BlockSpec((1, HEADS, tn, HEAD_PAD), lambda b, t: (b, 0, 0, 0)))
        in_specs.append(pl.BlockSpec((1, HEADS, V_EXT, tn), lambda b, t: (b, 0, 0, 0)))
        args += [k, v]
    in_specs.append(_layer_spec((1, MLA_WIDTH), l))
    args.append(g_att)
    return pl.pallas_call(
        functools.partial(_attn_kernel, nseg=nseg, tq=tq),
        out_shape=jax.ShapeDtypeStruct((B, S, MLA_WIDTH), BF16),
        grid=(B, S // tq_blk),
        in_specs=in_specs,
        out_specs=pl.BlockSpec((1, tq_blk, MLA_WIDTH), lambda b, t: (b, t, 0)),
        scratch_shapes=[pltpu.VMEM((ttot, tq), F32), pltpu.VMEM((ttot, tq), F32),
                        pltpu.VMEM((tq_blk // tq, HEADS, V_DIM, tq), F32)],
        compiler_params=pltpu.CompilerParams(
            dimension_semantics=("parallel", "arbitrary"), vmem_limit_bytes=VMEM_LIMIT),
        name="attn",
    )(*args)


def _out_kernel(x_ref, att_ref, rec_ref, sp_ref, mod_ref, n2_ref, wo_ref, wfi_ref, wfo_ref,
                fn_ref, o_ref, *, final):
    mod = mod_ref[...]
    y = _dot(att_ref[0], wo_ref[pl.ds(0, MLA_WIDTH), :])
    y = y + _dot(rec_ref[0], wo_ref[pl.ds(MLA_WIDTH, LRU_WIDTH), :])
    y = y + _dot(sp_ref[0], wo_ref[pl.ds(MLA_WIDTH + LRU_WIDTH, SGU_WIDTH), :])
    x1 = x_ref[0] + mod[2:3] * y
    h2 = (_rms(x1, n2_ref[...]) * (1.0 + mod[4:5]) + mod[3:4]).astype(BF16)
    acc = None
    for c0, cw in FF_CHUNKS:
        g = _dot(h2, wfi_ref[:, pl.ds(c0, cw)])
        u = _dot(h2, wfi_ref[:, pl.ds(D_FF + c0, cw)])
        act = (jax.nn.silu(g) * u).astype(BF16)
        part = _dot(act, wfo_ref[pl.ds(c0, cw), :])
        acc = part if acc is None else acc + part
    x2 = x1 + mod[5:6] * acc
    if final:
        x2 = _rms(x2, fn_ref[...])
    o_ref[0] = x2


def _out_call(x, att, rec, sp, mod, mod_row, lw, l, final_norm, final, tm):
    B, T, _ = x.shape
    tok = lambda w: pl.BlockSpec((1, tm, w), lambda b, t: (b, t, 0))
    W = functools.partial(_layer_spec, l=l)
    return pl.pallas_call(
        functools.partial(_out_kernel, final=final),
        out_shape=jax.ShapeDtypeStruct((B, T, D_MODEL), F32),
        grid=(B, T // tm),
        in_specs=[tok(D_MODEL), tok(MLA_WIDTH), tok(LRU_WIDTH), tok(SGU_WIDTH),
                  _mod_spec(l, mod_row),
                  W((1, D_MODEL)),
                  W((D_MODEL, D_MODEL)),
                  W((D_MODEL, 2 * D_FF)),
                  W((D_FF, D_MODEL)),
                  _const_spec((1, D_MODEL))],
        out_specs=tok(D_MODEL),
        compiler_params=pltpu.CompilerParams(
            dimension_semantics=("parallel", "parallel"), vmem_limit_bytes=VMEM_LIMIT),
        name="out_ffn",
    )(x, att, rec, sp, mod, lw["norm2"], lw["w_out"], lw["w_ffn_in"], lw["w_ffn_out"], final_norm)


def _rot_cols(w):
    q = ROPE // 4
    return jnp.concatenate([-w[..., q:2 * q], w[..., 0:q], -w[..., 3 * q:4 * q], w[..., 2 * q:3 * q]], axis=-1)


def _prep_params(p):
    L = DEPTH
    splits = np.cumsum([Q_LORA, KV_LORA, ROPE, LRU_WIDTH, LRU_WIDTH, SGU_WIDTH])
    qa, kva, kr, xr, gr, su, sv = jnp.split(p["w_in"], splits, axis=2)
    kr_blk = jnp.concatenate([kr, _rot_cols(kr), jnp.zeros((L, D_MODEL, LANE - 2 * ROPE), F32)], axis=2)
    w_in_ext = jnp.concatenate([qa, kva, kr_blk, xr, gr, su, sv], axis=2).astype(BF16)

    wq = p["w_q_b"].reshape(L, Q_LORA, HEADS, NOPE + ROPE)
    zpad = jnp.zeros((L, Q_LORA, HEADS, HEAD_PAD - NOPE - ROPE), F32)
    w_q = jnp.concatenate([wq, zpad], axis=-1).reshape(L, Q_LORA, HEADS * HEAD_PAD).astype(BF16)

    wkv = p["w_kv_b"].reshape(L, KV_LORA, HEADS, NOPE + V_DIM)
    wk = jnp.concatenate([wkv[..., :NOPE], jnp.zeros((L, KV_LORA, HEADS, HEAD_PAD - NOPE), F32)],
                         axis=-1).reshape(L, KV_LORA, HEADS * HEAD_PAD)
    wv = wkv[..., NOPE:].reshape(L, KV_LORA, MLA_WIDTH)
    place = np.zeros((LANE, HEADS, HEAD_PAD), np.float32)
    place[np.arange(ROPE), :, NOPE + np.arange(ROPE)] = 1.0
    place = jnp.broadcast_to(jnp.asarray(place.reshape(LANE, HEADS * HEAD_PAD)), (L, LANE, HEADS * HEAD_PAD))
    w_kv = jnp.concatenate([wk, place], axis=1).astype(BF16)
    w_vt = jnp.swapaxes(wv, 1, 2).astype(BF16)

    eye = np.eye(LRU_HEADS, dtype=np.float32)[None, :, None, :, None]

    def block_diag(w):
        _, hh, bi, bj = w.shape
        return (eye * w[:, :, :, None, :]).reshape(L, hh * bi, hh * bj)

    w_gates = jnp.concatenate([block_diag(p["lru_w_r"][:, 0]), block_diag(p["lru_w_i"][:, 0]),
                               block_diag(p["lru_w_r"][:, 1]), block_diag(p["lru_w_i"][:, 1])],
                              axis=2).astype(BF16)
    b_gates = jnp.concatenate([p["lru_b_r"][:, 0], p["lru_b_i"][:, 0],
                               p["lru_b_r"][:, 1], p["lru_b_i"][:, 1]], axis=1)[:, None]

    sgu_wcat = jnp.transpose(p["sgu_w"], (0, 2, 1, 3)).reshape(L, CHUNK, SGU_GROUPS * CHUNK).astype(BF16)
    sgu_bias = jnp.repeat(jnp.swapaxes(p["sgu_b"], 1, 2), SGU_GROUP_DIM, axis=2)
    gidx = np.arange(SGU_WIDTH) // SGU_GROUP_DIM
    gmat = jnp.asarray((gidx[:, None] == gidx[None, :]).astype(np.float32) / SGU_GROUP_DIM, dtype=BF16)

    on = p["out_norm"][:, None]
    row = lambda a: a[:, None]
    return {
        "norm1": row(p["norm1"]), "norm2": row(p["norm2"]),
        "w_in": w_in_ext, "q_a_norm": row(p["q_a_norm"]), "w_q": w_q,
        "kv_a_norm": row(p["kv_a_norm"]), "w_kv": w_kv, "w_vt": w_vt,
        "conv_w": p["conv_w"], "conv_b": row(p["conv_b"]),
        "w_gates": w_gates, "b_gates": b_gates, "lru_lam": p["lru_lam"],
        "sgu_wcat": sgu_wcat, "sgu_bias": sgu_bias, "sgu_norm": row(p["sgu_norm"]), "gmat": gmat,
        "g_att": on[..., :MLA_WIDTH], "g_rec": on[..., MLA_WIDTH:MLA_WIDTH + LRU_WIDTH],
        "g_sp": on[..., MLA_WIDTH + LRU_WIDTH:],
        "w_out": p["w_out"].astype(BF16), "w_ffn_in": p["w_ffn_in"].astype(BF16),
        "w_ffn_out": p["w_ffn_out"].astype(BF16),
    }


def _rope_tables(seq_len, ctx_len):
    f32 = np.float32
    rows = seq_len // GRID_W
    row = np.repeat(np.arange(rows, dtype=f32), GRID_W)
    col = np.tile(np.arange(GRID_W, dtype=f32), rows)
    half = ROPE // 2
    freq = (f32(ROPE_BASE) ** (-np.arange(0, half, 2, dtype=f32) / f32(half))).astype(f32)
    ar = row[:, None] * freq
    ac = col[:, None] * freq
    ang = np.concatenate([ar, ar, ac, ac], axis=-1).astype(f32)
    cos, sin = np.cos(ang).astype(f32), np.sin(ang).astype(f32)
    qs = f32(math.log2(math.e) / math.sqrt(NOPE + ROPE))

    up = ((np.arange(ROPE) // ROT_SHIFT) % 2 == 0).astype(f32)

    def q_tabs(c, s, n):
        pad = np.zeros((n, HEAD_PAD - NOPE - ROPE), f32)
        nope0 = np.zeros((n, NOPE), f32)
        cq = np.concatenate([np.ones((n, NOPE), f32), c, pad], axis=1)
        sa = np.concatenate([nope0, -s * up, pad], axis=1)
        sb = np.concatenate([nope0, s * (1 - up), pad], axis=1)
        tk = np.concatenate([c, s, np.zeros((n, LANE - 2 * ROPE), f32)], axis=1)
        return jnp.asarray(cq * qs), jnp.asarray(sa * qs), jnp.asarray(sb * qs), jnp.asarray(tk)

    lat = q_tabs(cos, sin, seq_len)
    ctx = q_tabs(np.ones((ctx_len, ROPE), f32), np.zeros((ctx_len, ROPE), f32), ctx_len)
    return lat, ctx


def kernel(x, c, ctx, c_ctx, norm1, norm2, w_ada, b_ada, w_in, q_a_norm, w_q_b, kv_a_norm, w_kv_b, conv_w, conv_b, lru_w_r, lru_b_r, lru_w_i, lru_b_i, lru_lam, sgu_norm, sgu_w, sgu_b, out_norm, w_out, w_ffn_in, w_ffn_out, final_norm):
    p = dict(norm1=norm1, norm2=norm2, w_in=w_in, q_a_norm=q_a_norm, w_q_b=w_q_b, kv_a_norm=kv_a_norm,
             w_kv_b=w_kv_b, conv_w=conv_w, conv_b=conv_b, lru_w_r=lru_w_r, lru_b_r=lru_b_r,
             lru_w_i=lru_w_i, lru_b_i=lru_b_i, lru_lam=lru_lam, sgu_norm=sgu_norm, sgu_w=sgu_w,
             sgu_b=sgu_b, out_norm=out_norm, w_out=w_out, w_ffn_in=w_ffn_in, w_ffn_out=w_ffn_out)
    B, S, _ = x.shape
    tctx = ctx.shape[1]
    mod_rows = ((B + 1 + SUBLANE - 1) // SUBLANE) * SUBLANE
    cc = jnp.concatenate([c, c_ctx[None], jnp.zeros((mod_rows - B - 1, D_MODEL), F32)], axis=0)
    mod = _ada_call(cc, w_ada, b_ada).reshape(DEPTH, mod_rows, 6, D_MODEL)
    tabs_l, tabs_c = _rope_tables(S, tctx)
    fn = final_norm[None]
    lw = _prep_params(p)

    h_ctx = ctx
    for l in range(DEPTH):
        last = l == DEPTH - 1
        q_l, k_l, v_l, xr_l, gr_l, sp_l = _inproj_call(x, mod, None, tabs_l, lw, l, 512)
        q_c, k_c, v_c, xr_c, gr_c, sp_c = _inproj_call(h_ctx, mod, B, tabs_c, lw, l, 256)
        rec_c, rec_l = _lru_call(xr_c, xr_l, gr_c, gr_l, lw, l)
        att_l = _attn_call(q_l, [(k_c, v_c), (k_l, v_l)], lw["g_att"], l, 1024, 256)
        x = _out_call(x, att_l, rec_l, sp_l, mod, None, lw, l, fn, last, 512)
        if not last:
            att_c = _attn_call(q_c, [(k_c, v_c)], lw["g_att"], l, 256, 256)
            h_ctx = _out_call(h_ctx, att_c, rec_c, sp_c, mod, B, lw, l, fn, False, 256)
    return x
```

```python
import functools
import math

import jax
import jax.numpy as jnp
import numpy as np
from jax import lax
from jax.experimental import pallas as pl
from jax.experimental.pallas import tpu as pltpu

F32 = jnp.float32
BF16 = jnp.bfloat16

D_MODEL = 1024
DEPTH = 2
GRID_W = 64
EPS = 1e-6
ROPE_BASE = 10000.0
HEADS = 8
NOPE = 64
ROPE = 32
V_DIM = 64
Q_LORA = 256
KV_LORA = 128
LRU_WIDTH = 256
LRU_HEADS = 4
CONV_W = 4
LRU_C = 8.0
SGU_GROUPS = 4
SGU_WIDTH = 256
SGU_GROUP_DIM = SGU_WIDTH // SGU_GROUPS
CHUNK = 128
D_FF = 2816
MLA_WIDTH = HEADS * V_DIM
V_EXT = V_DIM + 16

LANE = 128
SUBLANE = 8
HEAD_PAD = LANE
ROT_SHIFT = ROPE // 4
VMEM_LIMIT = 56 * 1024 * 1024

IN_EXT = 1536
C_QA, C_KVA, C_KR, C_XR, C_GR, C_SU, C_SV = 0, 256, 384, 512, 768, 1024, 1280
MXU_TILE = 256
FF_CHUNKS = ((0, 6 * MXU_TILE), (6 * MXU_TILE, 5 * MXU_TILE))


def _rms(x, g):
    ms = jnp.mean(x * x, axis=-1, keepdims=True)
    return x * lax.rsqrt(ms + EPS) * g


def _dot(a, b):
    return jnp.dot(a, b, preferred_element_type=F32)


def _const_spec(shape):
    zeros = (0,) * len(shape)
    return pl.BlockSpec(shape, lambda *_: zeros, pipeline_mode=pl.Buffered(1))


def _layer_spec(shape, l):
    zeros = (0,) * len(shape)
    return pl.BlockSpec((None,) + tuple(shape), lambda *_: (l,) + zeros, pipeline_mode=pl.Buffered(1))


def _mod_spec(l, row):
    if row is None:
        return pl.BlockSpec((None, None, 6, D_MODEL), lambda b, t: (l, b, 0, 0))
    return pl.BlockSpec((None, None, 6, D_MODEL), lambda b, t: (l, row, 0, 0))


ADA_TILE = 512


def _ada_kernel(c_ref, w_ref, b_ref, o_ref):
    s = jax.nn.silu(c_ref[...]).astype(BF16)
    o_ref[0] = _dot(s, w_ref[0].astype(BF16)) + b_ref[0]


def _ada_call(cc, w_ada, b_ada):
    rows = cc.shape[0]
    n = w_ada.shape[-1]
    return pl.pallas_call(
        _ada_kernel,
        out_shape=jax.ShapeDtypeStruct((DEPTH, rows, n), F32),
        grid=(DEPTH, n // ADA_TILE),
        in_specs=[
            pl.BlockSpec((rows, D_MODEL), lambda l, j: (0, 0)),
            pl.BlockSpec((1, D_MODEL, ADA_TILE), lambda l, j: (l, 0, j)),
            pl.BlockSpec((1, 1, ADA_TILE), lambda l, j: (l, 0, j)),
        ],
        out_specs=pl.BlockSpec((1, rows, ADA_TILE), lambda l, j: (l, 0, j)),
        compiler_params=pltpu.CompilerParams(
            dimension_semantics=("arbitrary", "arbitrary"), vmem_limit_bytes=VMEM_LIMIT),
        name="ada_mod",
    )(cc, w_ada, b_ada.reshape(DEPTH, 1, n))


def _inproj_kernel(x_ref, mod_ref, n1_ref, win_ref, qan_ref, wq_ref, kvn_ref, wkv_ref, wvt_ref,
                   cq_ref, sa_ref, sb_ref, tk_ref, wcat_ref, sbias_ref, sgn_ref, gmat_ref, gsp_ref,
                   q_ref, k_ref, v_ref, xr_ref, gr_ref, sp_ref, *, tm):
    x = x_ref[0]
    mod = mod_ref[...]
    h = _rms(x, n1_ref[...]) * (1.0 + mod[1:2]) + mod[0:1]
    z = _dot(h.astype(BF16), win_ref[...])

    qn = _rms(z[:, C_QA:C_QA + Q_LORA], qan_ref[...]).astype(BF16)
    qq = _dot(qn, wq_ref[...])
    cq = cq_ref[...]
    sa = sa_ref[...]
    sb = sb_ref[...]
    for hd in range(HEADS):
        qh = qq[:, hd * HEAD_PAD:(hd + 1) * HEAD_PAD]
        qh = (qh * cq + pltpu.roll(qh, LANE - ROT_SHIFT, axis=1) * sa
              + pltpu.roll(qh, ROT_SHIFT, axis=1) * sb)
        q_ref[0, hd] = qh.astype(BF16)

    kvn = _rms(z[:, C_KVA:C_KVA + KV_LORA], kvn_ref[...]).astype(BF16)
    t = z[:, C_KR:C_KR + LANE] * tk_ref[...]
    kro = t + pltpu.roll(t, LANE - ROPE, axis=1)
    lhs = jnp.concatenate([kvn, kro.astype(BF16)], axis=1)
    kk = _dot(lhs, wkv_ref[...])
    for hd in range(HEADS):
        k_ref[0, hd] = kk[:, hd * HEAD_PAD:(hd + 1) * HEAD_PAD].astype(BF16)
    vt = lax.dot_general(wvt_ref[...], kvn, (((1,), (1,)), ((), ())), preferred_element_type=F32)
    tail_row = lax.broadcasted_iota(jnp.int32, (V_EXT - V_DIM, tm), 0)
    tail = jnp.where(tail_row == 0, 1.0, 0.0).astype(BF16)
    for hd in range(HEADS):
        v_ref[0, hd] = jnp.concatenate([vt[hd * V_DIM:(hd + 1) * V_DIM].astype(BF16), tail], axis=0)

    xr_ref[...] = z[:, C_XR:C_XR + LRU_WIDTH]
    gr_ref[...] = z[:, C_GR:C_GR + LRU_WIDTH]

    u = jax.nn.gelu(z[:, C_SU:C_SU + SGU_WIDTH])
    vg = jax.nn.gelu(z[:, C_SV:C_SV + SGU_WIDTH])
    v2 = vg * vg
    v2_hi = v2.astype(BF16)
    v2_lo = (v2 - v2_hi.astype(F32)).astype(BF16)
    gmat = gmat_ref[...]
    gms = _dot(v2_hi, gmat) + _dot(v2_lo, gmat)
    vb = (vg * lax.rsqrt(gms + EPS) * sgn_ref[...]).astype(BF16)
    grp = lax.broadcasted_iota(jnp.int32, (CHUNK, SGU_WIDTH), 1) // SGU_GROUP_DIM
    zero = jnp.zeros((CHUNK, SGU_WIDTH), BF16)
    wcat = wcat_ref[...]
    sbias = sbias_ref[...]
    parts = []
    for c in range(tm // CHUNK):
        vc = vb[c * CHUNK:(c + 1) * CHUNK]
        rhs = jnp.concatenate([jnp.where(grp == g, vc, zero) for g in range(SGU_GROUPS)], axis=0)
        s = _dot(wcat, rhs) + sbias
        parts.append(u[c * CHUNK:(c + 1) * CHUNK] * s)
    sp = jnp.concatenate(parts, axis=0) if len(parts) > 1 else parts[0]
    sp_ref[0] = _rms(sp, gsp_ref[...]).astype(BF16)


def _inproj_call(x, mod, mod_row, tabs, lw, l, tm):
    B, T, _ = x.shape
    cq, sa, sb, tk = tabs
    tab_spec = pl.BlockSpec((tm, LANE), lambda b, t: (t, 0))
    W = functools.partial(_layer_spec, l=l)
    in_specs = [
        pl.BlockSpec((1, tm, D_MODEL), lambda b, t: (b, t, 0)),
        _mod_spec(l, mod_row),
        W((1, D_MODEL)),
        W((D_MODEL, IN_EXT)),
        W((1, Q_LORA)),
        W((Q_LORA, HEADS * HEAD_PAD)),
        W((1, KV_LORA)),
        W((2 * LANE, HEADS * HEAD_PAD)),
        W((MLA_WIDTH, KV_LORA)),
        tab_spec, tab_spec, tab_spec, tab_spec,
        W((CHUNK, SGU_GROUPS * CHUNK)),
        W((CHUNK, SGU_WIDTH)),
        W((1, SGU_WIDTH)),
        _const_spec((SGU_WIDTH, SGU_WIDTH)),
        W((1, SGU_WIDTH)),
    ]
    out_shape = [
        jax.ShapeDtypeStruct((B, HEADS, T, HEAD_PAD), BF16),
        jax.ShapeDtypeStruct((B, HEADS, T, HEAD_PAD), BF16),
        jax.ShapeDtypeStruct((B, HEADS, V_EXT, T), BF16),
        jax.ShapeDtypeStruct((T, B * LRU_WIDTH), F32),
        jax.ShapeDtypeStruct((T, B * LRU_WIDTH), F32),
        jax.ShapeDtypeStruct((B, T, SGU_WIDTH), BF16),
    ]
    out_specs = [
        pl.BlockSpec((1, HEADS, tm, HEAD_PAD), lambda b, t: (b, 0, t, 0)),
        pl.BlockSpec((1, HEADS, tm, HEAD_PAD), lambda b, t: (b, 0, t, 0)),
        pl.BlockSpec((1, HEADS, V_EXT, tm), lambda b, t: (b, 0, 0, t)),
        pl.BlockSpec((tm, LRU_WIDTH), lambda b, t: (t, b)),
        pl.BlockSpec((tm, LRU_WIDTH), lambda b, t: (t, b)),
        pl.BlockSpec((1, tm, SGU_WIDTH), lambda b, t: (b, t, 0)),
    ]
    return pl.pallas_call(
        functools.partial(_inproj_kernel, tm=tm),
        out_shape=out_shape,
        grid=(B, T // tm),
        in_specs=in_specs,
        out_specs=out_specs,
        compiler_params=pltpu.CompilerParams(
            dimension_semantics=("parallel", "parallel"), vmem_limit_bytes=VMEM_LIMIT),
        name="inproj",
    )(x, mod, lw["norm1"], lw["w_in"], lw["q_a_norm"], lw["w_q"], lw["kv_a_norm"], lw["w_kv"], lw["w_vt"],
      cq, sa, sb, tk, lw["sgu_wcat"], lw["sgu_bias"], lw["sgu_norm"], lw["gmat"], lw["g_sp"])


LRU_CHUNK = 128
LRU_SUB = 32
SCAN_UNROLL = 8


def _lru_chunk(s, reverse, n_ctx, n_lat):
    is_ctx = s < n_ctx
    if reverse:
        return is_ctx, jnp.where(is_ctx, n_ctx - 1 - s, n_lat - 1 - (s - n_ctx))
    return is_ctx, jnp.where(is_ctx, s, s - n_ctx)


def _lru_kernel(*refs, reverse, n_ctx, n_lat):
    xc_ref, xl_ref, pc_ref, pl_ref, nc_ref, nl_ref = refs[:6]
    if reverse:
        (hfc_ref, hfl_ref, grc_ref, grl_ref, cw_ref, cb_ref, wg_ref, bg_ref, lam_ref, grec_ref,
         outc_ref, outl_ref, a_ref, b_ref, h_ref) = refs[6:]
    else:
        (cw_ref, cb_ref, wg_ref, bg_ref, lam_ref, outc_ref, outl_ref, a_ref, b_ref, h_ref) = refs[6:]
    s = pl.program_id(0)
    is_ctx, c = _lru_chunk(s, reverse, n_ctx, n_lat)
    tc = a_ref.shape[0]
    d = 1 if reverse else 0

    @pl.when(s == 0)
    def _():
        h_ref[...] = jnp.zeros_like(h_ref)

    def pick(rc, rl, idx=slice(None)):
        return jnp.where(is_ctx, rc[idx], rl[idx])

    last = jnp.where(is_ctx, n_ctx - 1, n_lat - 1)
    prev2 = jnp.where(c != 0, pick(pc_ref, pl_ref, 0), 0.0)
    next1 = jnp.where(c != last, pick(nc_ref, nl_ref, 0)[0:1], 0.0)
    cw = cw_ref[...]
    half_nsp = (-0.5 * LRU_C * jax.nn.softplus(-lam_ref[...]))[d:d + 1]
    for k in range(tc // LRU_SUB):
        lo = k * LRU_SUB
        x = pick(xc_ref, xl_ref, pl.ds(lo, LRU_SUB))
        m2 = prev2 if k == 0 else pick(xc_ref, xl_ref, pl.ds(lo - 2, 2))
        p1 = next1 if lo + LRU_SUB == tc else pick(xc_ref, xl_ref, pl.ds(lo + LRU_SUB, 1))
        xc = (cb_ref[...] + cw[0:1] * jnp.concatenate([m2, x[:LRU_SUB - 2]], axis=0)
              + cw[1:2] * jnp.concatenate([m2[1:2], x[:LRU_SUB - 1]], axis=0)
              + cw[2:3] * x
              + cw[3:4] * jnp.concatenate([x[1:], p1], axis=0))
        nb = xc.shape[1]
        xc2 = xc.reshape(LRU_SUB * nb, LRU_WIDTH)
        g = _dot(xc2.astype(BF16), wg_ref[...]) + bg_ref[...]
        log_a = half_nsp * jnp.tanh(g[:, :LRU_WIDTH]) + half_nsp
        i = 0.5 * jnp.tanh(g[:, LRU_WIDTH:]) + 0.5
        a = jnp.exp(log_a)
        bb = jnp.sqrt(-jnp.tanh(log_a) * (a * a + 1.0)) * (i * xc2)
        a_ref[pl.ds(lo, LRU_SUB)] = a.reshape(LRU_SUB, nb, LRU_WIDTH)
        b_ref[pl.ds(lo, LRU_SUB)] = bb.reshape(LRU_SUB, nb, LRU_WIDTH)

    def scan_into(dst_ref):
        def body(j, h):
            base = (tc // SCAN_UNROLL - 1 - j if reverse else j) * SCAN_UNROLL
            for k in range(SCAN_UNROLL):
                t = base + (SCAN_UNROLL - 1 - k if reverse else k)
                h = a_ref[t] * h + b_ref[t]
                dst_ref[t] = h
            return h
        h_ref[...] = lax.fori_loop(0, tc // SCAN_UNROLL, body, h_ref[...])

    if reverse:
        scan_into(b_ref)
        grec = grec_ref[...]

        def finish(out_ref, hf_ref, gr_ref):
            for k in range(tc // LRU_SUB):
                sl = pl.ds(k * LRU_SUB, LRU_SUB)
                rec = (hf_ref[sl] + b_ref[sl]) * jax.nn.gelu(gr_ref[sl])
                out_ref[sl] = _rms(rec, grec)

        pl.when(is_ctx)(lambda: finish(outc_ref, hfc_ref, grc_ref))
        pl.when(jnp.logical_not(is_ctx))(lambda: finish(outl_ref, hfl_ref, grl_ref))
    else:
        pl.when(is_ctx)(lambda: scan_into(outc_ref))
        pl.when(jnp.logical_not(is_ctx))(lambda: scan_into(outl_ref))


def _lru_call(xr_c, xr_l, lw, l, reverse, hf=None, gr=None):
    tctx, tlat = xr_c.shape[0], xr_l.shape[0]
    nb = xr_c.shape[1] // LRU_WIDTH
    n_ctx, n_lat = tctx // LRU_CHUNK, tlat // LRU_CHUNK
    half = LRU_CHUNK // 2
    d = 1 if reverse else 0

    def seg(is_c, n):
        if is_c:
            park = 0 if reverse else n - 1
        else:
            park = n - 1 if reverse else 0

        def imap(s):
            is_ctx, c = _lru_chunk(s, reverse, n_ctx, n_lat)
            mine = is_ctx if is_c else jnp.logical_not(is_ctx)
            return jnp.where(mine, c, park)
        return imap

    def main(is_c, n):
        im = seg(is_c, n)
        return pl.BlockSpec((LRU_CHUNK, nb, LRU_WIDTH), lambda s: (im(s), 0, 0))

    def halo(is_c, n, nxt):
        im = seg(is_c, n)
        if nxt:
            return pl.BlockSpec((1, 2, nb, LRU_WIDTH),
                                lambda s: (jnp.minimum((im(s) + 1) * half, n * half - 1), 0, 0, 0))
        return pl.BlockSpec((1, 2, nb, LRU_WIDTH), lambda s: (jnp.maximum(im(s) * half - 1, 0), 0, 0, 0))

    as3 = lambda a: a.reshape(a.shape[0], nb, LRU_WIDTH)
    as4 = lambda a: a.reshape(a.shape[0] // 2, 2, nb, LRU_WIDTH)
    W = functools.partial(_layer_spec, l=l)
    args = [as3(xr_c), as3(xr_l), as4(xr_c), as4(xr_l), as4(xr_c), as4(xr_l)]
    in_specs = [main(True, n_ctx), main(False, n_lat), halo(True, n_ctx, False), halo(False, n_lat, False),
                halo(True, n_ctx, True), halo(False, n_lat, True)]
    if reverse:
        args += [hf[0], hf[1], as3(gr[0]), as3(gr[1])]
        in_specs += [main(True, n_ctx), main(False, n_lat), main(True, n_ctx), main(False, n_lat)]
    args += [lw["conv_w"], lw["conv_b"], lw["w_gates"], lw["b_gates"], lw["lru_lam"]]
    in_specs += [W((CONV_W, LRU_WIDTH)), W((1, LRU_WIDTH)),
                 pl.BlockSpec((None, LRU_WIDTH, 2 * LRU_WIDTH), lambda s: (l, 0, d)),
                 pl.BlockSpec((None, 1, 2 * LRU_WIDTH), lambda s: (l, 0, d)),
                 W((2, LRU_WIDTH))]
    if reverse:
        args.append(lw["g_rec"])
        in_specs.append(W((1, LRU_WIDTH)))
    return pl.pallas_call(
        functools.partial(_lru_kernel, reverse=reverse, n_ctx=n_ctx, n_lat=n_lat),
        out_shape=[jax.ShapeDtypeStruct((tctx, nb, LRU_WIDTH), F32),
                   jax.ShapeDtypeStruct((tlat, nb, LRU_WIDTH), F32)],
        grid=(n_ctx + n_lat,),
        in_specs=in_specs,
        out_specs=[main(True, n_ctx), main(False, n_lat)],
        scratch_shapes=[pltpu.VMEM((LRU_CHUNK, nb, LRU_WIDTH), F32),
                        pltpu.VMEM((LRU_CHUNK, nb, LRU_WIDTH), F32),
                        pltpu.VMEM((nb, LRU_WIDTH), F32)],
        compiler_params=pltpu.CompilerParams(
            dimension_semantics=("arbitrary",), vmem_limit_bytes=VMEM_LIMIT),
        name="lru_rev" if reverse else "lru_fwd",
    )(*args)


RED_ROWS = 64
ATTN_UNROLL = 8


def _col_reduce(x, op):
    rows, cols = x.shape
    part = op(x.reshape(rows // RED_ROWS, RED_ROWS, cols), axis=0)
    return op(part, axis=0, keepdims=True)


def _attn_kernel(*refs, nseg, tq):
    q_ref = refs[0]
    kv_refs = refs[1:1 + 2 * nseg]
    g_ref = refs[1 + 2 * nseg]
    o_ref = refs[2 + 2 * nseg]
    s_scr = refs[3 + 2 * nseg:5 + 2 * nseg]
    o_scr = refs[5 + 2 * nseg]
    seg_len = [kv_refs[2 * s].shape[2] for s in range(nseg)]
    seg_off = [sum(seg_len[:s]) for s in range(nseg)]

    nsub = q_ref.shape[2] // tq

    def scores_into(sub, hd, slot):
        q = q_ref[0, hd, pl.ds(pl.multiple_of(sub * tq, tq), tq), :]
        m = None
        for s in range(nseg):
            sc = lax.dot_general(kv_refs[2 * s][0, hd], q, (((1,), (1,)), ((), ())),
                                 preferred_element_type=F32)
            s_scr[slot][pl.ds(seg_off[s], seg_len[s]), :] = sc
            ms = _col_reduce(sc, jnp.max)
            m = ms if m is None else jnp.maximum(m, ms)
        return m

    def softmax_pv(sub, hd, slot, m):
        o = None
        for s in range(nseg):
            rows = pl.ds(seg_off[s], seg_len[s])
            p = jnp.exp2(s_scr[slot][rows, :] - m).astype(BF16)
            po = _dot(kv_refs[2 * s + 1][0, hd], p)
            o = po if o is None else o + po
        o_scr[sub, hd] = o[:V_DIM] * (1.0 / o[V_DIM:V_DIM + 1])

    def run_units(u0, n_units, m, has_next):
        for j in range(n_units):
            u = u0 + j
            if j + 1 < n_units or has_next:
                m_next = scores_into((u + 1) // HEADS, (u + 1) % HEADS, (j + 1) % 2)
            else:
                m_next = None
            softmax_pv(u // HEADS, u % HEADS, j % 2, m)
            m = m_next
        return m

    n_units = nsub * HEADS
    m_a = lax.fori_loop(0, n_units // ATTN_UNROLL - 1,
                        lambda i, m: run_units(i * ATTN_UNROLL, ATTN_UNROLL, m, True),
                        scores_into(0, 0, 0))
    run_units(n_units - ATTN_UNROLL, ATTN_UNROLL, m_a, False)
    for sub in range(nsub):
        att = o_scr[sub].reshape(MLA_WIDTH, tq).T
        o_ref[0, pl.ds(sub * tq, tq), :] = _rms(att, g_ref[...]).astype(BF16)


def _attn_call(q, segs, g_att, l, tq_blk, tq):
    B, _, S, _ = q.shape
    nseg = len(segs)
    ttot = sum(k.shape[2] for k, _ in segs)
    in_specs = [pl.BlockSpec((1, HEADS, tq_blk, HEAD_PAD), lambda b, t: (b, 0, t, 0))]
    args = [q]
    for k, v in segs:
        tn = k.shape[2]
        in_specs.append(pl.BlockSpec((1, HEADS, tn, HEAD_PAD), lambda b, t: (b, 0, 0, 0)))
        in_specs.append(pl.BlockSpec((1, HEADS, V_EXT, tn), lambda b, t: (b, 0, 0, 0)))
        args += [k, v]
    in_specs.append(_layer_spec((1, MLA_WIDTH), l))
    args.append(g_att)
    return pl.pallas_call(
        functools.partial(_attn_kernel, nseg=nseg, tq=tq),
        out_shape=jax.ShapeDtypeStruct((B, S, MLA_WIDTH), BF16),
        grid=(B, S // tq_blk),
        in_specs=in_specs,
        out_specs=pl.BlockSpec((1, tq_blk, MLA_WIDTH), lambda b, t: (b, t, 0)),
        scratch_shapes=[pltpu.VMEM((ttot, tq), F32), pltpu.VMEM((ttot, tq), F32),
                        pltpu.VMEM((tq_blk // tq, HEADS, V_DIM, tq), F32)],
        compiler_params=pltpu.CompilerParams(
            dimension_semantics=("parallel", "arbitrary"), vmem_limit_bytes=VMEM_LIMIT),
        name="attn",
    )(*args)


def _out_kernel(x_ref, att_ref, rec_ref, sp_ref, mod_ref, n2_ref, wo_ref, wfi_ref, wfo_ref,
                fn_ref, o_ref, *, final):
    mod = mod_ref[...]
    y = _dot(att_ref[0], wo_ref[pl.ds(0, MLA_WIDTH), :])
    y = y + _dot(rec_ref[...].astype(BF16), wo_ref[pl.ds(MLA_WIDTH, LRU_WIDTH), :])
    y = y + _dot(sp_ref[0], wo_ref[pl.ds(MLA_WIDTH + LRU_WIDTH, SGU_WIDTH), :])
    x1 = x_ref[0] + mod[2:3] * y
    h2 = (_rms(x1, n2_ref[...]) * (1.0 + mod[4:5]) + mod[3:4]).astype(BF16)
    acc = None
    for c0, cw in FF_CHUNKS:
        g = _dot(h2, wfi_ref[:, pl.ds(c0, cw)])
        u = _dot(h2, wfi_ref[:, pl.ds(D_FF + c0, cw)])
        act = (jax.nn.silu(g) * u).astype(BF16)
        part = _dot(act, wfo_ref[pl.ds(c0, cw), :])
        acc = part if acc is None else acc + part
    x2 = x1 + mod[5:6] * acc
    if final:
        x2 = _rms(x2, fn_ref[...])
    o_ref[0] = x2


def _out_call(x, att, rec, sp, mod, mod_row, lw, l, final_norm, final, tm):
    B, T, _ = x.shape
    tok = lambda w: pl.BlockSpec((1, tm, w), lambda b, t: (b, t, 0))
    W = functools.partial(_layer_spec, l=l)
    return pl.pallas_call(
        functools.partial(_out_kernel, final=final),
        out_shape=jax.ShapeDtypeStruct((B, T, D_MODEL), F32),
        grid=(B, T // tm),
        in_specs=[tok(D_MODEL), tok(MLA_WIDTH),
                  pl.BlockSpec((tm, LRU_WIDTH), lambda b, t: (t, b)),
                  tok(SGU_WIDTH),
                  _mod_spec(l, mod_row),
                  W((1, D_MODEL)),
                  W((D_MODEL, D_MODEL)),
                  W((D_MODEL, 2 * D_FF)),
                  W((D_FF, D_MODEL)),
                  _const_spec((1, D_MODEL))],
        out_specs=tok(D_MODEL),
        compiler_params=pltpu.CompilerParams(
            dimension_semantics=("parallel", "parallel"), vmem_limit_bytes=VMEM_LIMIT),
        name="out_ffn",
    )(x, att, rec, sp, mod, lw["norm2"], lw["w_out"], lw["w_ffn_in"], lw["w_ffn_out"], final_norm)


def _rot_cols(w):
    q = ROPE // 4
    return jnp.concatenate([-w[..., q:2 * q], w[..., 0:q], -w[..., 3 * q:4 * q], w[..., 2 * q:3 * q]], axis=-1)


def _prep_params(p):
    L = DEPTH
    splits = np.cumsum([Q_LORA, KV_LORA, ROPE, LRU_WIDTH, LRU_WIDTH, SGU_WIDTH])
    qa, kva, kr, xr, gr, su, sv = jnp.split(p["w_in"], splits, axis=2)
    kr_blk = jnp.concatenate([kr, _rot_cols(kr), jnp.zeros((L, D_MODEL, LANE - 2 * ROPE), F32)], axis=2)
    w_in_ext = jnp.concatenate([qa, kva, kr_blk, xr, gr, su, sv], axis=2).astype(BF16)

    wq = p["w_q_b"].reshape(L, Q_LORA, HEADS, NOPE + ROPE)
    zpad = jnp.zeros((L, Q_LORA, HEADS, HEAD_PAD - NOPE - ROPE), F32)
    w_q = jnp.concatenate([wq, zpad], axis=-1).reshape(L, Q_LORA, HEADS * HEAD_PAD).astype(BF16)

    wkv = p["w_kv_b"].reshape(L, KV_LORA, HEADS, NOPE + V_DIM)
    wk = jnp.concatenate([wkv[..., :NOPE], jnp.zeros((L, KV_LORA, HEADS, HEAD_PAD - NOPE), F32)],
                         axis=-1).reshape(L, KV_LORA, HEADS * HEAD_PAD)
    wv = wkv[..., NOPE:].reshape(L, KV_LORA, MLA_WIDTH)
    place = np.zeros((LANE, HEADS, HEAD_PAD), np.float32)
    place[np.arange(ROPE), :, NOPE + np.arange(ROPE)] = 1.0
    place = jnp.broadcast_to(jnp.asarray(place.reshape(LANE, HEADS * HEAD_PAD)), (L, LANE, HEADS * HEAD_PAD))
    w_kv = jnp.concatenate([wk, place], axis=1).astype(BF16)
    w_vt = jnp.swapaxes(wv, 1, 2).astype(BF16)

    eye = np.eye(LRU_HEADS, dtype=np.float32)[None, :, None, :, None]

    def block_diag(w):
        _, hh, bi, bj = w.shape
        return (eye * w[:, :, :, None, :]).reshape(L, hh * bi, hh * bj)

    w_gates = (0.5 * jnp.concatenate([block_diag(p["lru_w_r"][:, 0]), block_diag(p["lru_w_i"][:, 0]),
                                      block_diag(p["lru_w_r"][:, 1]), block_diag(p["lru_w_i"][:, 1])],
                                     axis=2)).astype(BF16)
    b_gates = 0.5 * jnp.concatenate([p["lru_b_r"][:, 0], p["lru_b_i"][:, 0],
                                     p["lru_b_r"][:, 1], p["lru_b_i"][:, 1]], axis=1)[:, None]

    sgu_wcat = jnp.transpose(p["sgu_w"], (0, 2, 1, 3)).reshape(L, CHUNK, SGU_GROUPS * CHUNK).astype(BF16)
    sgu_bias = jnp.repeat(jnp.swapaxes(p["sgu_b"], 1, 2), SGU_GROUP_DIM, axis=2)
    gidx = np.arange(SGU_WIDTH) // SGU_GROUP_DIM
    gmat = jnp.asarray((gidx[:, None] == gidx[None, :]).astype(np.float32) / SGU_GROUP_DIM, dtype=BF16)

    on = p["out_norm"][:, None]
    row = lambda a: a[:, None]
    return {
        "norm1": row(p["norm1"]), "norm2": row(p["norm2"]),
        "w_in": w_in_ext, "q_a_norm": row(p["q_a_norm"]), "w_q": w_q,
        "kv_a_norm": row(p["kv_a_norm"]), "w_kv": w_kv, "w_vt": w_vt,
        "conv_w": p["conv_w"], "conv_b": row(p["conv_b"]),
        "w_gates": w_gates, "b_gates": b_gates, "lru_lam": p["lru_lam"],
        "sgu_wcat": sgu_wcat, "sgu_bias": sgu_bias, "sgu_norm": row(p["sgu_norm"]), "gmat": gmat,
        "g_att": on[..., :MLA_WIDTH], "g_rec": on[..., MLA_WIDTH:MLA_WIDTH + LRU_WIDTH],
        "g_sp": on[..., MLA_WIDTH + LRU_WIDTH:],
        "w_out": p["w_out"].astype(BF16), "w_ffn_in": p["w_ffn_in"].astype(BF16),
        "w_ffn_out": p["w_ffn_out"].astype(BF16),
    }


def _rope_tables(seq_len, ctx_len):
    f32 = np.float32
    rows = seq_len // GRID_W
    row = np.repeat(np.arange(rows, dtype=f32), GRID_W)
    col = np.tile(np.arange(GRID_W, dtype=f32), rows)
    half = ROPE // 2
    freq = (f32(ROPE_BASE) ** (-np.arange(0, half, 2, dtype=f32) / f32(half))).astype(f32)
    ar = row[:, None] * freq
    ac = col[:, None] * freq
    ang = np.concatenate([ar, ar, ac, ac], axis=-1).astype(f32)
    cos, sin = np.cos(ang).astype(f32), np.sin(ang).astype(f32)
    qs = f32(math.log2(math.e) / math.sqrt(NOPE + ROPE))

    up = ((np.arange(ROPE) // ROT_SHIFT) % 2 == 0).astype(f32)

    def q_tabs(c, s, n):
        pad = np.zeros((n, HEAD_PAD - NOPE - ROPE), f32)
        nope0 = np.zeros((n, NOPE), f32)
        cq = np.concatenate([np.ones((n, NOPE), f32), c, pad], axis=1)
        sa = np.concatenate([nope0, -s * up, pad], axis=1)
        sb = np.concatenate([nope0, s * (1 - up), pad], axis=1)
        tk = np.concatenate([c, s, np.zeros((n, LANE - 2 * ROPE), f32)], axis=1)
        return jnp.asarray(cq * qs), jnp.asarray(sa * qs), jnp.asarray(sb * qs), jnp.asarray(tk)

    lat = q_tabs(cos, sin, seq_len)
    ctx = q_tabs(np.ones((ctx_len, ROPE), f32), np.zeros((ctx_len, ROPE), f32), ctx_len)
    return lat, ctx


def kernel(x, c, ctx, c_ctx, norm1, norm2, w_ada, b_ada, w_in, q_a_norm, w_q_b, kv_a_norm, w_kv_b, conv_w, conv_b, lru_w_r, lru_b_r, lru_w_i, lru_b_i, lru_lam, sgu_norm, sgu_w, sgu_b, out_norm, w_out, w_ffn_in, w_ffn_out, final_norm):
    p = dict(norm1=norm1, norm2=norm2, w_in=w_in, q_a_norm=q_a_norm, w_q_b=w_q_b, kv_a_norm=kv_a_norm,
             w_kv_b=w_kv_b, conv_w=conv_w, conv_b=conv_b, lru_w_r=lru_w_r, lru_b_r=lru_b_r,
             lru_w_i=lru_w_i, lru_b_i=lru_b_i, lru_lam=lru_lam, sgu_norm=sgu_norm, sgu_w=sgu_w,
             sgu_b=sgu_b, out_norm=out_norm, w_out=w_out, w_ffn_in=w_ffn_in, w_ffn_out=w_ffn_out)
    B, S, _ = x.shape
    tctx = ctx.shape[1]
    mod_rows = ((B + 1 + SUBLANE - 1) // SUBLANE) * SUBLANE
    cc = jnp.concatenate([c, c_ctx[None], jnp.zeros((mod_rows - B - 1, D_MODEL), F32)], axis=0)
    mod = _ada_call(cc, w_ada, b_ada).reshape(DEPTH, mod_rows, 6, D_MODEL)
    tabs_l, tabs_c = _rope_tables(S, tctx)
    fn = final_norm[None]
    lw = _prep_params(p)

    h_ctx = ctx
    for l in range(DEPTH):
        last = l == DEPTH - 1
        q_l, k_l, v_l, xr_l, gr_l, sp_l = _inproj_call(x, mod, None, tabs_l, lw, l, 512)
        q_c, k_c, v_c, xr_c, gr_c, sp_c = _inproj_call(h_ctx, mod, B, tabs_c, lw, l, 256)
        hf = _lru_call(xr_c, xr_l, lw, l, False)
        rec_c, rec_l = [r.reshape(r.shape[0], B * LRU_WIDTH)
                        for r in _lru_call(xr_c, xr_l, lw, l, True, hf=hf, gr=(gr_c, gr_l))]
        att_l = _attn_call(q_l, [(k_c, v_c), (k_l, v_l)], lw["g_att"], l, 1024, 256)
        x = _out_call(x, att_l, rec_l, sp_l, mod, None, lw, l, fn, last, 512)
        if not last:
            att_c = _attn_call(q_c, [(k_c, v_c)], lw["g_att"], l, 256, 256)
            h_ctx = _out_call(h_ctx, att_c, rec_c, sp_c, mod, B, lw, l, fn, False, 256)
    return x
```

```python
import functools
import math

import jax
import jax.numpy as jnp
import numpy as np
from jax import lax
from jax.experimental import pallas as pl
from jax.experimental.pallas import tpu as pltpu

F32 = jnp.float32
BF16 = jnp.bfloat16

D_MODEL = 1024
DEPTH = 2
GRID_W = 64
EPS = 1e-6
ROPE_BASE = 10000.0
HEADS = 8
NOPE = 64
ROPE = 32
V_DIM = 64
Q_LORA = 256
KV_LORA = 128
LRU_WIDTH = 256
LRU_HEADS = 4
CONV_W = 4
LRU_C = 8.0
SGU_GROUPS = 4
SGU_WIDTH = 256
SGU_GROUP_DIM = SGU_WIDTH // SGU_GROUPS
CHUNK = 128
D_FF = 2816
MLA_WIDTH = HEADS * V_DIM
V_EXT = V_DIM + 16

LANE = 128
SUBLANE = 8
HEAD_PAD = LANE
ROT_SHIFT = ROPE // 4
VMEM_LIMIT = 56 * 1024 * 1024

IN_EXT = 1536
C_QA, C_KVA, C_KR, C_XR, C_GR, C_SU, C_SV = 0, 256, 384, 512, 768, 1024, 1280
MXU_TILE = 256
FF_CHUNKS = ((0, 6 * MXU_TILE), (6 * MXU_TILE, 5 * MXU_TILE))
OUT_SPLIT = 2
IN_SPLIT = 1


def _rms(x, g):
    ms = jnp.mean(x * x, axis=-1, keepdims=True)
    return x * lax.rsqrt(ms + EPS) * g


def _dot(a, b):
    return jnp.dot(a, b, preferred_element_type=F32)


def _const_spec(shape):
    zeros = (0,) * len(shape)
    return pl.BlockSpec(shape, lambda *_: zeros, pipeline_mode=pl.Buffered(1))


def _layer_spec(shape, l):
    zeros = (0,) * len(shape)
    return pl.BlockSpec((None,) + tuple(shape), lambda *_: (l,) + zeros, pipeline_mode=pl.Buffered(1))


def _mod_spec(l, row):
    if row is None:
        return pl.BlockSpec((None, None, 6, D_MODEL), lambda b, t: (l, b, 0, 0))
    return pl.BlockSpec((None, None, 6, D_MODEL), lambda b, t: (l, row, 0, 0))


ADA_TILE = 512


def _ada_kernel(c_ref, w_ref, b_ref, o_ref):
    s = jax.nn.silu(c_ref[...]).astype(BF16)
    o_ref[0] = _dot(s, w_ref[0].astype(BF16)) + b_ref[0]


def _ada_call(cc, w_ada, b_ada):
    rows = cc.shape[0]
    n = w_ada.shape[-1]
    return pl.pallas_call(
        _ada_kernel,
        out_shape=jax.ShapeDtypeStruct((DEPTH, rows, n), F32),
        grid=(DEPTH, n // ADA_TILE),
        in_specs=[
            pl.BlockSpec((rows, D_MODEL), lambda l, j: (0, 0)),
            pl.BlockSpec((1, D_MODEL, ADA_TILE), lambda l, j: (l, 0, j)),
            pl.BlockSpec((1, 1, ADA_TILE), lambda l, j: (l, 0, j)),
        ],
        out_specs=pl.BlockSpec((1, rows, ADA_TILE), lambda l, j: (l, 0, j)),
        compiler_params=pltpu.CompilerParams(
            dimension_semantics=("arbitrary", "arbitrary"), vmem_limit_bytes=VMEM_LIMIT),
        name="ada_mod",
    )(cc, w_ada, b_ada.reshape(DEPTH, 1, n))


def _inproj_kernel(x_ref, mod_ref, n1_ref, win_ref, qan_ref, wq_ref, kvn_ref, wkv_ref, wvt_ref,
                   cq_ref, sa_ref, sb_ref, tk_ref, wcat_ref, sbias_ref, sgn_ref, gmat_ref, gsp_ref,
                   q_ref, k_ref, v_ref, xr_ref, gr_ref, sp_ref, *, tm):
    mod = mod_ref[...]
    rows = tm // IN_SPLIT
    grp = lax.broadcasted_iota(jnp.int32, (CHUNK, SGU_WIDTH), 1) // SGU_GROUP_DIM
    zero = jnp.zeros((CHUNK, SGU_WIDTH), BF16)
    tail_row = lax.broadcasted_iota(jnp.int32, (V_EXT - V_DIM, rows), 0)
    tail = jnp.where(tail_row == 0, 1.0, 0.0).astype(BF16)

    for r in range(IN_SPLIT):
        sl = pl.ds(r * rows, rows)
        h = _rms(x_ref[0, sl, :], n1_ref[...]) * (1.0 + mod[1:2]) + mod[0:1]
        z = _dot(h.astype(BF16), win_ref[...])

        qn = _rms(z[:, C_QA:C_QA + Q_LORA], qan_ref[...]).astype(BF16)
        qq = _dot(qn, wq_ref[...])
        cq = cq_ref[sl, :]
        sa = sa_ref[sl, :]
        sb = sb_ref[sl, :]
        for hd in range(HEADS):
            qh = qq[:, hd * HEAD_PAD:(hd + 1) * HEAD_PAD]
            qh = (qh * cq + pltpu.roll(qh, LANE - ROT_SHIFT, axis=1) * sa
                  + pltpu.roll(qh, ROT_SHIFT, axis=1) * sb)
            q_ref[0, hd, sl, :] = qh.astype(BF16)

        kvn = _rms(z[:, C_KVA:C_KVA + KV_LORA], kvn_ref[...]).astype(BF16)
        t = z[:, C_KR:C_KR + LANE] * tk_ref[sl, :]
        kro = t + pltpu.roll(t, LANE - ROPE, axis=1)
        lhs = jnp.concatenate([kvn, kro.astype(BF16)], axis=1)
        kk = _dot(lhs, wkv_ref[...])
        for hd in range(HEADS):
            k_ref[0, hd, sl, :] = kk[:, hd * HEAD_PAD:(hd + 1) * HEAD_PAD].astype(BF16)
        vt = lax.dot_general(wvt_ref[...], kvn, (((1,), (1,)), ((), ())), preferred_element_type=F32)
        for hd in range(HEADS):
            v_ref[0, hd, :, sl] = jnp.concatenate([vt[hd * V_DIM:(hd + 1) * V_DIM].astype(BF16), tail], axis=0)

        xr_ref[0, sl, :] = z[:, C_XR:C_XR + LRU_WIDTH]
        gr_ref[0, sl, :] = z[:, C_GR:C_GR + LRU_WIDTH]

        u = jax.nn.gelu(z[:, C_SU:C_SU + SGU_WIDTH])
        vg = jax.nn.gelu(z[:, C_SV:C_SV + SGU_WIDTH])
        v2 = vg * vg
        v2_hi = v2.astype(BF16)
        v2_lo = (v2 - v2_hi.astype(F32)).astype(BF16)
        gmat = gmat_ref[...]
        gms = _dot(v2_hi, gmat) + _dot(v2_lo, gmat)
        vb = (vg * lax.rsqrt(gms + EPS) * sgn_ref[...]).astype(BF16)
        parts = []
        for c in range(rows // CHUNK):
            vc = vb[c * CHUNK:(c + 1) * CHUNK]
            rhs = jnp.concatenate([jnp.where(grp == g, vc, zero) for g in range(SGU_GROUPS)], axis=0)
            s = _dot(wcat_ref[...], rhs) + sbias_ref[...]
            parts.append(u[c * CHUNK:(c + 1) * CHUNK] * s)
        sp = jnp.concatenate(parts, axis=0) if len(parts) > 1 else parts[0]
        sp_ref[0, sl, :] = _rms(sp, gsp_ref[...]).astype(BF16)


def _inproj_call(x, mod, mod_row, tabs, lw, l, tm):
    B, T, _ = x.shape
    cq, sa, sb, tk = tabs
    tab_spec = pl.BlockSpec((tm, LANE), lambda b, t: (t, 0))
    W = functools.partial(_layer_spec, l=l)
    in_specs = [
        pl.BlockSpec((1, tm, D_MODEL), lambda b, t: (b, t, 0)),
        _mod_spec(l, mod_row),
        W((1, D_MODEL)),
        W((D_MODEL, IN_EXT)),
        W((1, Q_LORA)),
        W((Q_LORA, HEADS * HEAD_PAD)),
        W((1, KV_LORA)),
        W((2 * LANE, HEADS * HEAD_PAD)),
        W((MLA_WIDTH, KV_LORA)),
        tab_spec, tab_spec, tab_spec, tab_spec,
        W((CHUNK, SGU_GROUPS * CHUNK)),
        W((CHUNK, SGU_WIDTH)),
        W((1, SGU_WIDTH)),
        _const_spec((SGU_WIDTH, SGU_WIDTH)),
        W((1, SGU_WIDTH)),
    ]
    out_shape = [
        jax.ShapeDtypeStruct((B, HEADS, T, HEAD_PAD), BF16),
        jax.ShapeDtypeStruct((B, HEADS, T, HEAD_PAD), BF16),
        jax.ShapeDtypeStruct((B, HEADS, V_EXT, T), BF16),
        jax.ShapeDtypeStruct((B, T, LRU_WIDTH), F32),
        jax.ShapeDtypeStruct((B, T, LRU_WIDTH), F32),
        jax.ShapeDtypeStruct((B, T, SGU_WIDTH), BF16),
    ]
    out_specs = [
        pl.BlockSpec((1, HEADS, tm, HEAD_PAD), lambda b, t: (b, 0, t, 0)),
        pl.BlockSpec((1, HEADS, tm, HEAD_PAD), lambda b, t: (b, 0, t, 0)),
        pl.BlockSpec((1, HEADS, V_EXT, tm), lambda b, t: (b, 0, 0, t)),
        pl.BlockSpec((1, tm, LRU_WIDTH), lambda b, t: (b, t, 0)),
        pl.BlockSpec((1, tm, LRU_WIDTH), lambda b, t: (b, t, 0)),
        pl.BlockSpec((1, tm, SGU_WIDTH), lambda b, t: (b, t, 0)),
    ]
    return pl.pallas_call(
        functools.partial(_inproj_kernel, tm=tm),
        out_shape=out_shape,
        grid=(B, T // tm),
        in_specs=in_specs,
        out_specs=out_specs,
        compiler_params=pltpu.CompilerParams(
            dimension_semantics=("parallel", "parallel"), vmem_limit_bytes=VMEM_LIMIT),
        name="inproj",
    )(x, mod, lw["norm1"], lw["w_in"], lw["q_a_norm"], lw["w_q"], lw["kv_a_norm"], lw["w_kv"], lw["w_vt"],
      cq, sa, sb, tk, lw["sgu_wcat"], lw["sgu_bias"], lw["sgu_norm"], lw["gmat"], lw["g_sp"])


GATE_ROWS = 256
PAD_ROWS = SUBLANE
SCAN_UNROLL = 4


def _tile_scan(a, b, h_prev, row, reverse):
    for k in (1, 2, 4):
        if reverse:
            keep = row < SUBLANE - k
            sh = SUBLANE - k
        else:
            keep = row >= k
            sh = k
        a_s = jnp.where(keep, pltpu.roll(a, sh, axis=0), 1.0)
        b_s = jnp.where(keep, pltpu.roll(b, sh, axis=0), 0.0)
        b = a * b_s + b
        a = a * a_s
    h = a * h_prev + b
    last = h[0:1] if reverse else h[SUBLANE - 1:SUBLANE]
    return h, jnp.broadcast_to(last, h.shape)


def _lru_kernel(xrc_ref, xrl_ref, grc_ref, grl_ref, cw_ref, cb_ref, wg_ref, bg_ref, lam_ref,
                grec_ref, outc_ref, outl_ref, pad_ref, xc_ref, a_ref, b_ref, hs_ref, *, tc, tl):
    cw = cw_ref[...]
    cb = cb_ref[...]
    ttot = tc + tl

    def conv(src_ref, n, dst0):
        pad_ref[pl.ds(0, PAD_ROWS), :] = jnp.zeros((PAD_ROWS, LRU_WIDTH), F32)
        pad_ref[pl.ds(PAD_ROWS, n), :] = src_ref[0]
        pad_ref[pl.ds(PAD_ROWS + n, PAD_ROWS), :] = jnp.zeros((PAD_ROWS, LRU_WIDTH), F32)
        y = cb
        for j in range(CONV_W):
            y = y + cw[j:j + 1] * pad_ref[pl.ds(PAD_ROWS - CONV_W // 2 + j, n), :]
        xc_ref[pl.ds(dst0, n), :] = y

    conv(xrc_ref, tc, 0)
    conv(xrl_ref, tl, tc)

    half_nsp = -0.5 * LRU_C * jax.nn.softplus(-lam_ref[...])
    wg = wg_ref[...]
    bg = bg_ref[...]
    tiles_per_chunk = GATE_ROWS // SUBLANE
    for c in range(ttot // GATE_ROWS):
        xcc = xc_ref[pl.ds(c * GATE_ROWS, GATE_ROWS), :]
        g = _dot(xcc.astype(BF16), wg) + bg
        for d in range(2):
            tr = jnp.tanh(g[:, (2 * d) * LRU_WIDTH:(2 * d + 1) * LRU_WIDTH])
            i = 0.5 * jnp.tanh(g[:, (2 * d + 1) * LRU_WIDTH:(2 * d + 2) * LRU_WIDTH]) + 0.5
            log_a = half_nsp[d:d + 1] * tr + half_nsp[d:d + 1]
            a = jnp.exp(log_a)
            bb = jnp.sqrt(-jnp.tanh(log_a) * (a * a + 1.0)) * (i * xcc)
            a_ref[d, pl.ds(c * tiles_per_chunk, tiles_per_chunk)] = a.reshape(
                tiles_per_chunk, SUBLANE, LRU_WIDTH)
            b_ref[d, pl.ds(c * tiles_per_chunk, tiles_per_chunk)] = bb.reshape(
                tiles_per_chunk, SUBLANE, LRU_WIDTH)

    row = lax.broadcasted_iota(jnp.int32, (SUBLANE, LRU_WIDTH), 0)
    nc = tc // SUBLANE
    nl = tl // SUBLANE

    def run(first_f, first_r, n, hf, hr):
        def body(i, carry):
            hf, hr = carry
            for j in range(SCAN_UNROLL):
                tf = first_f + i * SCAN_UNROLL + j
                tr = first_r - i * SCAN_UNROLL - j
                of, hf = _tile_scan(a_ref[0, tf], b_ref[0, tf], hf, row, False)
                orv, hr = _tile_scan(a_ref[1, tr], b_ref[1, tr], hr, row, True)
                hs_ref[0, tf] = of
                hs_ref[1, tr] = orv
            return hf, hr
        return lax.fori_loop(0, n // SCAN_UNROLL, body, (hf, hr))

    zero = jnp.zeros((SUBLANE, LRU_WIDTH), F32)
    hf, hr = run(0, nc - 1, nc, zero, zero)
    run(nc, nc + nl - 1, nl, hf, hr)

    grec = grec_ref[...]
    for c in range(ttot // GATE_ROWS):
        sl = pl.ds(c * tiles_per_chunk, tiles_per_chunk)
        hsum = (hs_ref[0, sl] + hs_ref[1, sl]).reshape(GATE_ROWS, LRU_WIDTH)
        r0 = c * GATE_ROWS
        if r0 < tc:
            gate = grc_ref[0, pl.ds(r0, GATE_ROWS), :]
        else:
            gate = grl_ref[0, pl.ds(r0 - tc, GATE_ROWS), :]
        y = _rms(hsum * jax.nn.gelu(gate), grec).astype(BF16)
        if r0 < tc:
            outc_ref[0, pl.ds(r0, GATE_ROWS), :] = y
        else:
            outl_ref[0, pl.ds(r0 - tc, GATE_ROWS), :] = y


def _lru_call(xr_c, xr_l, gr_c, gr_l, lw, l):
    B, tc, _ = xr_c.shape
    tl = xr_l.shape[1]
    ttot = tc + tl
    assert tc % GATE_ROWS == 0 and tl % GATE_ROWS == 0
    seq = lambda n: pl.BlockSpec((1, n, LRU_WIDTH), lambda b: (b, 0, 0))
    W = functools.partial(_layer_spec, l=l)
    return pl.pallas_call(
        functools.partial(_lru_kernel, tc=tc, tl=tl),
        out_shape=[jax.ShapeDtypeStruct((B, tc, LRU_WIDTH), BF16),
                   jax.ShapeDtypeStruct((B, tl, LRU_WIDTH), BF16)],
        grid=(B,),
        in_specs=[seq(tc), seq(tl), seq(tc), seq(tl),
                  W((CONV_W, LRU_WIDTH)), W((1, LRU_WIDTH)),
                  W((LRU_WIDTH, 4 * LRU_WIDTH)), W((1, 4 * LRU_WIDTH)),
                  W((2, LRU_WIDTH)), W((1, LRU_WIDTH))],
        out_specs=[seq(tc), seq(tl)],
        scratch_shapes=[
            pltpu.VMEM((tl + 2 * PAD_ROWS, LRU_WIDTH), F32),
            pltpu.VMEM((ttot, LRU_WIDTH), F32),
            pltpu.VMEM((2, ttot // SUBLANE, SUBLANE, LRU_WIDTH), F32),
            pltpu.VMEM((2, ttot // SUBLANE, SUBLANE, LRU_WIDTH), F32),
            pltpu.VMEM((2, ttot // SUBLANE, SUBLANE, LRU_WIDTH), F32),
        ],
        compiler_params=pltpu.CompilerParams(
            dimension_semantics=("parallel",), vmem_limit_bytes=VMEM_LIMIT),
        name="lru",
    )(xr_c, xr_l, gr_c, gr_l, lw["conv_w"], lw["conv_b"], lw["w_gates"], lw["b_gates"],
      lw["lru_lam"], lw["g_rec"])


RED_ROWS = 64
ATTN_UNROLL = 8


def _col_reduce(x, op):
    rows, cols = x.shape
    part = op(x.reshape(rows // RED_ROWS, RED_ROWS, cols), axis=0)
    return op(part, axis=0, keepdims=True)


def _attn_kernel(*refs, nseg, tq):
    q_ref = refs[0]
    kv_refs = refs[1:1 + 2 * nseg]
    g_ref = refs[1 + 2 * nseg]
    o_ref = refs[2 + 2 * nseg]
    s_scr = refs[3 + 2 * nseg:5 + 2 * nseg]
    o_scr = refs[5 + 2 * nseg]
    seg_len = [kv_refs[2 * s].shape[2] for s in range(nseg)]
    seg_off = [sum(seg_len[:s]) for s in range(nseg)]

    nsub = q_ref.shape[2] // tq

    def scores_into(sub, hd, slot):
        q = q_ref[0, hd, pl.ds(pl.multiple_of(sub * tq, tq), tq), :]
        m = None
        for s in range(nseg):
            sc = lax.dot_general(kv_refs[2 * s][0, hd], q, (((1,), (1,)), ((), ())),
                                 preferred_element_type=F32)
            s_scr[slot][pl.ds(seg_off[s], seg_len[s]), :] = sc
            ms = _col_reduce(sc, jnp.max)
            m = ms if m is None else jnp.maximum(m, ms)
        return m

    def softmax_pv(sub, hd, slot, m):
        o = None
        for s in range(nseg):
            rows = pl.ds(seg_off[s], seg_len[s])
            p = jnp.exp2(s_scr[slot][rows, :] - m).astype(BF16)
            po = _dot(kv_refs[2 * s + 1][0, hd], p)
            o = po if o is None else o + po
        o_scr[sub, hd] = o[:V_DIM] * (1.0 / o[V_DIM:V_DIM + 1])

    def run_units(u0, n_units, m, has_next):
        for j in range(n_units):
            u = u0 + j
            if j + 1 < n_units or has_next:
                m_next = scores_into((u + 1) // HEADS, (u + 1) % HEADS, (j + 1) % 2)
            else:
                m_next = None
            softmax_pv(u // HEADS, u % HEADS, j % 2, m)
            m = m_next
        return m

    n_units = nsub * HEADS
    m_a = lax.fori_loop(0, n_units // ATTN_UNROLL - 1,
                        lambda i, m: run_units(i * ATTN_UNROLL, ATTN_UNROLL, m, True),
                        scores_into(0, 0, 0))
    run_units(n_units - ATTN_UNROLL, ATTN_UNROLL, m_a, False)
    for sub in range(nsub):
        att = o_scr[sub].reshape(MLA_WIDTH, tq).T
        o_ref[0, pl.ds(sub * tq, tq), :] = _rms(att, g_ref[...]).astype(BF16)


def _attn_call(q, segs, g_att, l, tq_blk, tq):
    B, _, S, _ = q.shape
    nseg = len(segs)
    ttot = sum(k.shape[2] for k, _ in segs)
    in_specs = [pl.BlockSpec((1, HEADS, tq_blk, HEAD_PAD), lambda b, t: (b, 0, t, 0))]
    args = [q]
    for k, v in segs:
        tn = k.shape[2]
        in_specs.append(pl.BlockSpec((1, HEADS, tn, HEAD_PAD), lambda b, t: (b, 0, 0, 0)))
        in_specs.append(pl.BlockSpec((1, HEADS, V_EXT, tn), lambda b, t: (b, 0, 0, 0)))
        args += [k, v]
    in_specs.append(_layer_spec((1, MLA_WIDTH), l))
    args.append(g_att)
    return pl.pallas_call(
        functools.partial(_attn_kernel, nseg=nseg, tq=tq),
        out_shape=jax.ShapeDtypeStruct((B, S, MLA_WIDTH), BF16),
        grid=(B, S // tq_blk),
        in_specs=in_specs,
        out_specs=pl.BlockSpec((1, tq_blk, MLA_WIDTH), lambda b, t: (b, t, 0)),
        scratch_shapes=[pltpu.VMEM((ttot, tq), F32), pltpu.VMEM((ttot, tq), F32),
                        pltpu.VMEM((tq_blk // tq, HEADS, V_DIM, tq), F32)],
        compiler_params=pltpu.CompilerParams(
            dimension_semantics=("parallel", "arbitrary"), vmem_limit_bytes=VMEM_LIMIT),
        name="attn",
    )(*args)


def _out_kernel(x_ref, att_ref, rec_ref, sp_ref, mod_ref, n2_ref, wo_ref, wfi_ref, wfo_ref,
                fn_ref, o_ref, *, final):
    mod = mod_ref[...]
    rows = x_ref.shape[1] // OUT_SPLIT
    x1s, h2s = [], []
    for r in range(OUT_SPLIT):
        sl = pl.ds(r * rows, rows)
        y = _dot(att_ref[0, sl, :], wo_ref[pl.ds(0, MLA_WIDTH), :])
        y = y + _dot(rec_ref[0, sl, :], wo_ref[pl.ds(MLA_WIDTH, LRU_WIDTH), :])
        y = y + _dot(sp_ref[0, sl, :], wo_ref[pl.ds(MLA_WIDTH + LRU_WIDTH, SGU_WIDTH), :])
        x1 = x_ref[0, sl, :] + mod[2:3] * y
        x1s.append(x1)
        h2s.append((_rms(x1, n2_ref[...]) * (1.0 + mod[4:5]) + mod[3:4]).astype(BF16))
    for r in range(OUT_SPLIT):
        acc = None
        for c0, cw in FF_CHUNKS:
            g = _dot(h2s[r], wfi_ref[:, pl.ds(c0, cw)])
            u = _dot(h2s[r], wfi_ref[:, pl.ds(D_FF + c0, cw)])
            act = (jax.nn.silu(g) * u).astype(BF16)
            part = _dot(act, wfo_ref[pl.ds(c0, cw), :])
            acc = part if acc is None else acc + part
        x2 = x1s[r] + mod[5:6] * acc
        if final:
            x2 = _rms(x2, fn_ref[...])
        o_ref[0, pl.ds(r * rows, rows), :] = x2


def _out_call(x, att, rec, sp, mod, mod_row, lw, l, final_norm, final, tm):
    B, T, _ = x.shape
    tok = lambda w: pl.BlockSpec((1, tm, w), lambda b, t: (b, t, 0))
    W = functools.partial(_layer_spec, l=l)
    return pl.pallas_call(
        functools.partial(_out_kernel, final=final),
        out_shape=jax.ShapeDtypeStruct((B, T, D_MODEL), F32),
        grid=(B, T // tm),
        in_specs=[tok(D_MODEL), tok(MLA_WIDTH), tok(LRU_WIDTH), tok(SGU_WIDTH),
                  _mod_spec(l, mod_row),
                  W((1, D_MODEL)),
                  W((D_MODEL, D_MODEL)),
                  W((D_MODEL, 2 * D_FF)),
                  W((D_FF, D_MODEL)),
                  _const_spec((1, D_MODEL))],
        out_specs=tok(D_MODEL),
        compiler_params=pltpu.CompilerParams(
            dimension_semantics=("parallel", "parallel"), vmem_limit_bytes=VMEM_LIMIT),
        name="out_ffn",
    )(x, att, rec, sp, mod, lw["norm2"], lw["w_out"], lw["w_ffn_in"], lw["w_ffn_out"], final_norm)


def _rot_cols(w):
    q = ROPE // 4
    return jnp.concatenate([-w[..., q:2 * q], w[..., 0:q], -w[..., 3 * q:4 * q], w[..., 2 * q:3 * q]], axis=-1)


def _prep_params(p):
    L = DEPTH
    splits = np.cumsum([Q_LORA, KV_LORA, ROPE, LRU_WIDTH, LRU_WIDTH, SGU_WIDTH])
    qa, kva, kr, xr, gr, su, sv = jnp.split(p["w_in"].astype(BF16), splits, axis=2)
    kr_blk = jnp.concatenate([kr, _rot_cols(kr), jnp.zeros((L, D_MODEL, LANE - 2 * ROPE), BF16)], axis=2)
    w_in_ext = jnp.concatenate([qa, kva, kr_blk, xr, gr, su, sv], axis=2)

    wq = p["w_q_b"].astype(BF16).reshape(L, Q_LORA, HEADS, NOPE + ROPE)
    zpad = jnp.zeros((L, Q_LORA, HEADS, HEAD_PAD - NOPE - ROPE), BF16)
    w_q = jnp.concatenate([wq, zpad], axis=-1).reshape(L, Q_LORA, HEADS * HEAD_PAD)

    wkv = p["w_kv_b"].astype(BF16).reshape(L, KV_LORA, HEADS, NOPE + V_DIM)
    wk = jnp.concatenate([wkv[..., :NOPE], jnp.zeros((L, KV_LORA, HEADS, HEAD_PAD - NOPE), BF16)],
                         axis=-1).reshape(L, KV_LORA, HEADS * HEAD_PAD)
    wv = wkv[..., NOPE:].reshape(L, KV_LORA, MLA_WIDTH)
    place = np.zeros((LANE, HEADS, HEAD_PAD), np.float32)
    place[np.arange(ROPE), :, NOPE + np.arange(ROPE)] = 1.0
    place = jnp.broadcast_to(jnp.asarray(place.reshape(LANE, HEADS * HEAD_PAD), dtype=BF16),
                             (L, LANE, HEADS * HEAD_PAD))
    w_kv = jnp.concatenate([wk, place], axis=1)
    w_vt = jnp.swapaxes(wv, 1, 2)

    eye = np.eye(LRU_HEADS, dtype=np.float32)[None, :, None, :, None]

    def block_diag(w):
        _, hh, bi, bj = w.shape
        return (eye * w[:, :, :, None, :]).reshape(L, hh * bi, hh * bj)

    w_gates = (0.5 * jnp.concatenate([block_diag(p["lru_w_r"][:, 0]), block_diag(p["lru_w_i"][:, 0]),
                                      block_diag(p["lru_w_r"][:, 1]), block_diag(p["lru_w_i"][:, 1])],
                                     axis=2)).astype(BF16)
    b_gates = 0.5 * jnp.concatenate([p["lru_b_r"][:, 0], p["lru_b_i"][:, 0],
                                     p["lru_b_r"][:, 1], p["lru_b_i"][:, 1]], axis=1)[:, None]

    sgu_wcat = jnp.transpose(p["sgu_w"], (0, 2, 1, 3)).reshape(L, CHUNK, SGU_GROUPS * CHUNK).astype(BF16)
    sgu_bias = jnp.repeat(jnp.swapaxes(p["sgu_b"], 1, 2), SGU_GROUP_DIM, axis=2)
    gidx = np.arange(SGU_WIDTH) // SGU_GROUP_DIM
    gmat = jnp.asarray((gidx[:, None] == gidx[None, :]).astype(np.float32) / SGU_GROUP_DIM, dtype=BF16)

    on = p["out_norm"][:, None]
    row = lambda a: a[:, None]
    return {
        "norm1": row(p["norm1"]), "norm2": row(p["norm2"]),
        "w_in": w_in_ext, "q_a_norm": row(p["q_a_norm"]), "w_q": w_q,
        "kv_a_norm": row(p["kv_a_norm"]), "w_kv": w_kv, "w_vt": w_vt,
        "conv_w": p["conv_w"], "conv_b": row(p["conv_b"]),
        "w_gates": w_gates, "b_gates": b_gates, "lru_lam": p["lru_lam"],
        "sgu_wcat": sgu_wcat, "sgu_bias": sgu_bias, "sgu_norm": row(p["sgu_norm"]), "gmat": gmat,
        "g_att": on[..., :MLA_WIDTH], "g_rec": on[..., MLA_WIDTH:MLA_WIDTH + LRU_WIDTH],
        "g_sp": on[..., MLA_WIDTH + LRU_WIDTH:],
        "w_out": p["w_out"].astype(BF16), "w_ffn_in": p["w_ffn_in"].astype(BF16),
        "w_ffn_out": p["w_ffn_out"].astype(BF16),
    }


def _rope_tables(seq_len, ctx_len):
    f32 = np.float32
    rows = seq_len // GRID_W
    row = np.repeat(np.arange(rows, dtype=f32), GRID_W)
    col = np.tile(np.arange(GRID_W, dtype=f32), rows)
    half = ROPE // 2
    freq = (f32(ROPE_BASE) ** (-np.arange(0, half, 2, dtype=f32) / f32(half))).astype(f32)
    ar = row[:, None] * freq
    ac = col[:, None] * freq
    ang = np.concatenate([ar, ar, ac, ac], axis=-1).astype(f32)
    cos, sin = np.cos(ang).astype(f32), np.sin(ang).astype(f32)
    qs = f32(math.log2(math.e) / math.sqrt(NOPE + ROPE))

    up = ((np.arange(ROPE) // ROT_SHIFT) % 2 == 0).astype(f32)

    def q_tabs(c, s, n):
        pad = np.zeros((n, HEAD_PAD - NOPE - ROPE), f32)
        nope0 = np.zeros((n, NOPE), f32)
        cq = np.concatenate([np.ones((n, NOPE), f32), c, pad], axis=1)
        sa = np.concatenate([nope0, -s * up, pad], axis=1)
        sb = np.concatenate([nope0, s * (1 - up), pad], axis=1)
        tk = np.concatenate([c, s, np.zeros((n, LANE - 2 * ROPE), f32)], axis=1)
        return jnp.asarray(cq * qs), jnp.asarray(sa * qs), jnp.asarray(sb * qs), jnp.asarray(tk)

    lat = q_tabs(cos, sin, seq_len)
    ctx = q_tabs(np.ones((ctx_len, ROPE), f32), np.zeros((ctx_len, ROPE), f32), ctx_len)
    return lat, ctx


def kernel(x, c, ctx, c_ctx, norm1, norm2, w_ada, b_ada, w_in, q_a_norm, w_q_b, kv_a_norm, w_kv_b, conv_w, conv_b, lru_w_r, lru_b_r, lru_w_i, lru_b_i, lru_lam, sgu_norm, sgu_w, sgu_b, out_norm, w_out, w_ffn_in, w_ffn_out, final_norm):
    p = dict(norm1=norm1, norm2=norm2, w_in=w_in, q_a_norm=q_a_norm, w_q_b=w_q_b, kv_a_norm=kv_a_norm,
             w_kv_b=w_kv_b, conv_w=conv_w, conv_b=conv_b, lru_w_r=lru_w_r, lru_b_r=lru_b_r,
             lru_w_i=lru_w_i, lru_b_i=lru_b_i, lru_lam=lru_lam, sgu_norm=sgu_norm, sgu_w=sgu_w,
             sgu_b=sgu_b, out_norm=out_norm, w_out=w_out, w_ffn_in=w_ffn_in, w_ffn_out=w_ffn_out)
    B, S, _ = x.shape
    tctx = ctx.shape[1]
    mod_rows = ((B + 1 + SUBLANE - 1) // SUBLANE) * SUBLANE
    cc = jnp.concatenate([c, c_ctx[None], jnp.zeros((mod_rows - B - 1, D_MODEL), F32)], axis=0)
    mod = _ada_call(cc, w_ada, b_ada).reshape(DEPTH, mod_rows, 6, D_MODEL)
    tabs_l, tabs_c = _rope_tables(S, tctx)
    fn = final_norm[None]
    lw = _prep_params(p)

    h_ctx = ctx
    for l in range(DEPTH):
        last = l == DEPTH - 1
        q_l, k_l, v_l, xr_l, gr_l, sp_l = _inproj_call(x, mod, None, tabs_l, lw, l, 512)
        q_c, k_c, v_c, xr_c, gr_c, sp_c = _inproj_call(h_ctx, mod, B, tabs_c, lw, l, 256)
        rec_c, rec_l = _lru_call(xr_c, xr_l, gr_c, gr_l, lw, l)
        att_l = _attn_call(q_l, [(k_c, v_c), (k_l, v_l)], lw["g_att"], l, 1024, 256)
        x = _out_call(x, att_l, rec_l, sp_l, mod, None, lw, l, fn, last, 512)
        if not last:
            att_c = _attn_call(q_c, [(k_c, v_c)], lw["g_att"], l, 256, 256)
            h_ctx = _out_call(h_ctx, att_c, rec_c, sp_c, mod, B, lw, l, fn, False, 256)
    return x
```

```python
import functools
import math

import jax
import jax.numpy as jnp
import numpy as np
from jax import lax
from jax.experimental import pallas as pl
from jax.experimental.pallas import tpu as pltpu

F32 = jnp.float32
BF16 = jnp.bfloat16

D_MODEL = 1024
DEPTH = 2
GRID_W = 64
EPS = 1e-6
ROPE_BASE = 10000.0
HEADS = 8
NOPE = 64
ROPE = 32
V_DIM = 64
Q_LORA = 256
KV_LORA = 128
LRU_WIDTH = 256
LRU_HEADS = 4
CONV_W = 4
LRU_C = 8.0
SGU_GROUPS = 4
SGU_WIDTH = 256
SGU_GROUP_DIM = SGU_WIDTH // SGU_GROUPS
CHUNK = 128
D_FF = 2816
MLA_WIDTH = HEADS * V_DIM
V_EXT = V_DIM + 16

LANE = 128
SUBLANE = 8
HEAD_PAD = LANE
ROT_SHIFT = ROPE // 4
VMEM_LIMIT = 56 * 1024 * 1024

IN_EXT = 1536
C_QA, C_KVA, C_KR, C_XR, C_GR, C_SU, C_SV = 0, 256, 384, 512, 768, 1024, 1280
MXU_TILE = 256
FF_CHUNKS = ((0, 6 * MXU_TILE), (6 * MXU_TILE, 5 * MXU_TILE))
OUT_SPLIT = 2
NORM_ROWS = 128


def _rms(x, g):
    ms = jnp.mean(x * x, axis=-1, keepdims=True)
    return x * lax.rsqrt(ms + EPS) * g


def _dot(a, b):
    return jnp.dot(a, b, preferred_element_type=F32)


def _const_spec(shape):
    zeros = (0,) * len(shape)
    return pl.BlockSpec(shape, lambda *_: zeros, pipeline_mode=pl.Buffered(1))


def _layer_spec(shape, l):
    zeros = (0,) * len(shape)
    return pl.BlockSpec((None,) + tuple(shape), lambda *_: (l,) + zeros, pipeline_mode=pl.Buffered(1))


def _mod_spec(l, row):
    if row is None:
        return pl.BlockSpec((None, None, 6, D_MODEL), lambda b, t: (l, b, 0, 0))
    return pl.BlockSpec((None, None, 6, D_MODEL), lambda b, t: (l, row, 0, 0))


ADA_TILE = 512


def _ada_kernel(c_ref, w_ref, b_ref, o_ref):
    s = jax.nn.silu(c_ref[...]).astype(BF16)
    o_ref[0] = _dot(s, w_ref[0].astype(BF16)) + b_ref[0]


def _ada_call(cc, w_ada, b_ada):
    rows = cc.shape[0]
    n = w_ada.shape[-1]
    return pl.pallas_call(
        _ada_kernel,
        out_shape=jax.ShapeDtypeStruct((DEPTH, rows, n), F32),
        grid=(DEPTH, n // ADA_TILE),
        in_specs=[
            pl.BlockSpec((rows, D_MODEL), lambda l, j: (0, 0)),
            pl.BlockSpec((1, D_MODEL, ADA_TILE), lambda l, j: (l, 0, j)),
            pl.BlockSpec((1, 1, ADA_TILE), lambda l, j: (l, 0, j)),
        ],
        out_specs=pl.BlockSpec((1, rows, ADA_TILE), lambda l, j: (l, 0, j)),
        compiler_params=pltpu.CompilerParams(
            dimension_semantics=("arbitrary", "arbitrary"), vmem_limit_bytes=VMEM_LIMIT),
        name="ada_mod",
    )(cc, w_ada, b_ada.reshape(DEPTH, 1, n))


def _inproj_kernel(x_ref, mod_ref, n1_ref, win_ref, qan_ref, wq_ref, kvn_ref, wkv_ref, wvt_ref,
                   cq_ref, sa_ref, sb_ref, tk_ref, wcat_ref, sbias_ref, sgn_ref, gmat_ref, gsp_ref,
                   q_ref, k_ref, v_ref, xr_ref, gr_ref, sp_ref, *, tm):
    mod = mod_ref[...]
    zs = []
    for k in range(tm // NORM_ROWS):
        xs = x_ref[0, pl.ds(k * NORM_ROWS, NORM_ROWS), :]
        h = _rms(xs, n1_ref[...]) * (1.0 + mod[1:2]) + mod[0:1]
        zs.append(_dot(h.astype(BF16), win_ref[...]))
    z = jnp.concatenate(zs, axis=0)

    qn = _rms(z[:, C_QA:C_QA + Q_LORA], qan_ref[...]).astype(BF16)
    qq = _dot(qn, wq_ref[...])
    cq = cq_ref[...]
    sa = sa_ref[...]
    sb = sb_ref[...]
    for hd in range(HEADS):
        qh = qq[:, hd * HEAD_PAD:(hd + 1) * HEAD_PAD]
        qh = (qh * cq + pltpu.roll(qh, LANE - ROT_SHIFT, axis=1) * sa
              + pltpu.roll(qh, ROT_SHIFT, axis=1) * sb)
        q_ref[0, hd] = qh.astype(BF16)

    kvn = _rms(z[:, C_KVA:C_KVA + KV_LORA], kvn_ref[...]).astype(BF16)
    t = z[:, C_KR:C_KR + LANE] * tk_ref[...]
    kro = t + pltpu.roll(t, LANE - ROPE, axis=1)
    lhs = jnp.concatenate([kvn, kro.astype(BF16)], axis=1)
    kk = _dot(lhs, wkv_ref[...])
    for hd in range(HEADS):
        k_ref[0, hd] = kk[:, hd * HEAD_PAD:(hd + 1) * HEAD_PAD].astype(BF16)
    vt = lax.dot_general(wvt_ref[...], kvn, (((1,), (1,)), ((), ())), preferred_element_type=F32)
    tail_row = lax.broadcasted_iota(jnp.int32, (V_EXT - V_DIM, tm), 0)
    tail = jnp.where(tail_row == 0, 1.0, 0.0).astype(BF16)
    for hd in range(HEADS):
        v_ref[0, hd] = jnp.concatenate([vt[hd * V_DIM:(hd + 1) * V_DIM].astype(BF16), tail], axis=0)

    xr_ref[0] = z[:, C_XR:C_XR + LRU_WIDTH]
    gr_ref[0] = z[:, C_GR:C_GR + LRU_WIDTH]

    u = jax.nn.gelu(z[:, C_SU:C_SU + SGU_WIDTH])
    vg = jax.nn.gelu(z[:, C_SV:C_SV + SGU_WIDTH])
    v2 = vg * vg
    v2_hi = v2.astype(BF16)
    v2_lo = (v2 - v2_hi.astype(F32)).astype(BF16)
    gmat = gmat_ref[...]
    gms = _dot(v2_hi, gmat) + _dot(v2_lo, gmat)
    vb = (vg * lax.rsqrt(gms + EPS) * sgn_ref[...]).astype(BF16)
    grp = lax.broadcasted_iota(jnp.int32, (CHUNK, SGU_WIDTH), 1) // SGU_GROUP_DIM
    zero = jnp.zeros((CHUNK, SGU_WIDTH), BF16)
    wcat = wcat_ref[...]
    sbias = sbias_ref[...]
    parts = []
    for c in range(tm // CHUNK):
        vc = vb[c * CHUNK:(c + 1) * CHUNK]
        rhs = jnp.concatenate([jnp.where(grp == g, vc, zero) for g in range(SGU_GROUPS)], axis=0)
        s = _dot(wcat, rhs) + sbias
        parts.append(u[c * CHUNK:(c + 1) * CHUNK] * s)
    sp = jnp.concatenate(parts, axis=0) if len(parts) > 1 else parts[0]
    sp_ref[0] = _rms(sp, gsp_ref[...]).astype(BF16)


def _inproj_call(x, mod, mod_row, tabs, lw, l, tm):
    B, T, _ = x.shape
    cq, sa, sb, tk = tabs
    tab_spec = pl.BlockSpec((tm, LANE), lambda b, t: (t, 0))
    W = functools.partial(_layer_spec, l=l)
    in_specs = [
        pl.BlockSpec((1, tm, D_MODEL), lambda b, t: (b, t, 0)),
        _mod_spec(l, mod_row),
        W((1, D_MODEL)),
        W((D_MODEL, IN_EXT)),
        W((1, Q_LORA)),
        W((Q_LORA, HEADS * HEAD_PAD)),
        W((1, KV_LORA)),
        W((2 * LANE, HEADS * HEAD_PAD)),
        W((MLA_WIDTH, KV_LORA)),
        tab_spec, tab_spec, tab_spec, tab_spec,
        W((CHUNK, SGU_GROUPS * CHUNK)),
        W((CHUNK, SGU_WIDTH)),
        W((1, SGU_WIDTH)),
        _const_spec((SGU_WIDTH, SGU_WIDTH)),
        W((1, SGU_WIDTH)),
    ]
    out_shape = [
        jax.ShapeDtypeStruct((B, HEADS, T, HEAD_PAD), BF16),
        jax.ShapeDtypeStruct((B, HEADS, T, HEAD_PAD), BF16),
        jax.ShapeDtypeStruct((B, HEADS, V_EXT, T), BF16),
        jax.ShapeDtypeStruct((B, T, LRU_WIDTH), F32),
        jax.ShapeDtypeStruct((B, T, LRU_WIDTH), F32),
        jax.ShapeDtypeStruct((B, T, SGU_WIDTH), BF16),
    ]
    out_specs = [
        pl.BlockSpec((1, HEADS, tm, HEAD_PAD), lambda b, t: (b, 0, t, 0)),
        pl.BlockSpec((1, HEADS, tm, HEAD_PAD), lambda b, t: (b, 0, t, 0)),
        pl.BlockSpec((1, HEADS, V_EXT, tm), lambda b, t: (b, 0, 0, t)),
        pl.BlockSpec((1, tm, LRU_WIDTH), lambda b, t: (b, t, 0)),
        pl.BlockSpec((1, tm, LRU_WIDTH), lambda b, t: (b, t, 0)),
        pl.BlockSpec((1, tm, SGU_WIDTH), lambda b, t: (b, t, 0)),
    ]
    return pl.pallas_call(
        functools.partial(_inproj_kernel, tm=tm),
        out_shape=out_shape,
        grid=(B, T // tm),
        in_specs=in_specs,
        out_specs=out_specs,
        compiler_params=pltpu.CompilerParams(
            dimension_semantics=("parallel", "parallel"), vmem_limit_bytes=VMEM_LIMIT),
        name="inproj",
    )(x, mod, lw["norm1"], lw["w_in"], lw["q_a_norm"], lw["w_q"], lw["kv_a_norm"], lw["w_kv"], lw["w_vt"],
      cq, sa, sb, tk, lw["sgu_wcat"], lw["sgu_bias"], lw["sgu_norm"], lw["gmat"], lw["g_sp"])


GATE_ROWS = 256
SCAN_UNROLL = 8


def _tile_scan(a, b, h_prev, row, reverse):
    for k in (1, 2, 4):
        if reverse:
            keep = row < SUBLANE - k
            sh = SUBLANE - k
        else:
            keep = row >= k
            sh = k
        a_s = jnp.where(keep, pltpu.roll(a, sh, axis=0), 1.0)
        b_s = jnp.where(keep, pltpu.roll(b, sh, axis=0), 0.0)
        b = a * b_s + b
        a = a * a_s
    h = a * h_prev + b
    last = h[0:1] if reverse else h[SUBLANE - 1:SUBLANE]
    return h, jnp.broadcast_to(last, h.shape)


def _lru_kernel(xrc_ref, xrl_ref, cw_ref, cb_ref, wg_ref, bg_ref, lam_ref,
                outc_ref, outl_ref, xc_ref, a_ref, b_ref, hs_ref, *, tc, tl):
    cw = cw_ref[...]
    cb = cb_ref[...]
    ttot = tc + tl

    def conv(src_ref, n, dst0):
        x = src_ref[0]
        row = lax.broadcasted_iota(jnp.int32, (n, LRU_WIDTH), 0)
        y = cb
        for j in range(CONV_W):
            off = j - CONV_W // 2
            if off == 0:
                tap = x
            else:
                inside = row >= -off if off < 0 else row < n - off
                tap = jnp.where(inside, pltpu.roll(x, (-off) % n, axis=0), 0.0)
            y = y + cw[j:j + 1] * tap
        xc_ref[pl.ds(dst0, n), :] = y

    conv(xrc_ref, tc, 0)
    conv(xrl_ref, tl, tc)

    half_nsp = -0.5 * LRU_C * jax.nn.softplus(-lam_ref[...])
    wg = wg_ref[...]
    bg = bg_ref[...]
    tiles_per_chunk = GATE_ROWS // SUBLANE
    for c in range(ttot // GATE_ROWS):
        xcc = xc_ref[pl.ds(c * GATE_ROWS, GATE_ROWS), :]
        g = _dot(xcc.astype(BF16), wg) + bg
        for d in range(2):
            tr = jnp.tanh(g[:, (2 * d) * LRU_WIDTH:(2 * d + 1) * LRU_WIDTH])
            i = 0.5 * jnp.tanh(g[:, (2 * d + 1) * LRU_WIDTH:(2 * d + 2) * LRU_WIDTH]) + 0.5
            log_a = half_nsp[d:d + 1] * tr + half_nsp[d:d + 1]
            a = jnp.exp(log_a)
            bb = jnp.sqrt(-jnp.tanh(log_a) * (a * a + 1.0)) * (i * xcc)
            a_ref[d, pl.ds(c * tiles_per_chunk, tiles_per_chunk)] = a.reshape(
                tiles_per_chunk, SUBLANE, LRU_WIDTH)
            b_ref[d, pl.ds(c * tiles_per_chunk, tiles_per_chunk)] = bb.reshape(
                tiles_per_chunk, SUBLANE, LRU_WIDTH)

    row = lax.broadcasted_iota(jnp.int32, (SUBLANE, LRU_WIDTH), 0)
    nc = tc // SUBLANE
    nl = tl // SUBLANE

    def run(first_f, first_r, n, hf, hr):
        def body(i, carry):
            hf, hr = carry
            for j in range(SCAN_UNROLL):
                tf = first_f + i * SCAN_UNROLL + j
                tr = first_r - i * SCAN_UNROLL - j
                of, hf = _tile_scan(a_ref[0, tf], b_ref[0, tf], hf, row, False)
                orv, hr = _tile_scan(a_ref[1, tr], b_ref[1, tr], hr, row, True)
                hs_ref[0, tf] = of
                hs_ref[1, tr] = orv
            return hf, hr
        return lax.fori_loop(0, n // SCAN_UNROLL, body, (hf, hr))

    zero = jnp.zeros((SUBLANE, LRU_WIDTH), F32)
    hf, hr = run(0, nc - 1, nc, zero, zero)
    run(nc, nc + nl - 1, nl, hf, hr)

    for c in range(ttot // GATE_ROWS):
        sl = pl.ds(c * tiles_per_chunk, tiles_per_chunk)
        hsum = (hs_ref[0, sl] + hs_ref[1, sl]).reshape(GATE_ROWS, LRU_WIDTH)
        r0 = c * GATE_ROWS
        if r0 < tc:
            outc_ref[0, pl.ds(r0, GATE_ROWS), :] = hsum
        else:
            outl_ref[0, pl.ds(r0 - tc, GATE_ROWS), :] = hsum


def _lru_call(xr_c, xr_l, lw, l):
    B, tc, _ = xr_c.shape
    tl = xr_l.shape[1]
    ttot = tc + tl
    assert tc % GATE_ROWS == 0 and tl % GATE_ROWS == 0
    seq = lambda n: pl.BlockSpec((1, n, LRU_WIDTH), lambda b: (b, 0, 0))
    W = functools.partial(_layer_spec, l=l)
    return pl.pallas_call(
        functools.partial(_lru_kernel, tc=tc, tl=tl),
        out_shape=[jax.ShapeDtypeStruct((B, tc, LRU_WIDTH), F32),
                   jax.ShapeDtypeStruct((B, tl, LRU_WIDTH), F32)],
        grid=(B,),
        in_specs=[seq(tc), seq(tl),
                  W((CONV_W, LRU_WIDTH)), W((1, LRU_WIDTH)),
                  W((LRU_WIDTH, 4 * LRU_WIDTH)), W((1, 4 * LRU_WIDTH)),
                  W((2, LRU_WIDTH))],
        out_specs=[seq(tc), seq(tl)],
        scratch_shapes=[
            pltpu.VMEM((ttot, LRU_WIDTH), F32),
            pltpu.VMEM((2, ttot // SUBLANE, SUBLANE, LRU_WIDTH), F32),
            pltpu.VMEM((2, ttot // SUBLANE, SUBLANE, LRU_WIDTH), F32),
            pltpu.VMEM((2, ttot // SUBLANE, SUBLANE, LRU_WIDTH), F32),
        ],
        compiler_params=pltpu.CompilerParams(
            dimension_semantics=("parallel",), vmem_limit_bytes=VMEM_LIMIT),
        name="lru",
    )(xr_c, xr_l, lw["conv_w"], lw["conv_b"], lw["w_gates"], lw["b_gates"], lw["lru_lam"])


RED_ROWS = 64
ATTN_UNROLL = 8


def _col_reduce(x, op):
    rows, cols = x.shape
    part = op(x.reshape(rows // RED_ROWS, RED_ROWS, cols), axis=0)
    return op(part, axis=0, keepdims=True)


def _attn_kernel(*refs, nseg, tq):
    q_ref = refs[0]
    kv_refs = refs[1:1 + 2 * nseg]
    g_ref = refs[1 + 2 * nseg]
    o_ref = refs[2 + 2 * nseg]
    s_scr = refs[3 + 2 * nseg:5 + 2 * nseg]
    o_scr = refs[5 + 2 * nseg]
    seg_len = [kv_refs[2 * s].shape[2] for s in range(nseg)]
    seg_off = [sum(seg_len[:s]) for s in range(nseg)]

    nsub = q_ref.shape[2] // tq

    def scores_into(sub, hd, slot):
        q = q_ref[0, hd, pl.ds(pl.multiple_of(sub * tq, tq), tq), :]
        m = None
        for s in range(nseg):
            sc = lax.dot_general(kv_refs[2 * s][0, hd], q, (((1,), (1,)), ((), ())),
                                 preferred_element_type=F32)
            s_scr[slot][pl.ds(seg_off[s], seg_len[s]), :] = sc
            ms = _col_reduce(sc, jnp.max)
            m = ms if m is None else jnp.maximum(m, ms)
        return m

    def softmax_pv(sub, hd, slot, m):
        o = None
        for s in range(nseg):
            rows = pl.ds(seg_off[s], seg_len[s])
            p = jnp.exp2(s_scr[slot][rows, :] - m).astype(BF16)
            po = _dot(kv_refs[2 * s + 1][0, hd], p)
            o = po if o is None else o + po
        o_scr[sub, hd] = o[:V_DIM] * (1.0 / o[V_DIM:V_DIM + 1])

    def run_units(u0, n_units, m, has_next):
        for j in range(n_units):
            u = u0 + j
            if j + 1 < n_units or has_next:
                m_next = scores_into((u + 1) // HEADS, (u + 1) % HEADS, (j + 1) % 2)
            else:
                m_next = None
            softmax_pv(u // HEADS, u % HEADS, j % 2, m)
            m = m_next
        return m

    n_units = nsub * HEADS
    m_a = lax.fori_loop(0, n_units // ATTN_UNROLL - 1,
                        lambda i, m: run_units(i * ATTN_UNROLL, ATTN_UNROLL, m, True),
                        scores_into(0, 0, 0))
    run_units(n_units - ATTN_UNROLL, ATTN_UNROLL, m_a, False)
    for sub in range(nsub):
        att = o_scr[sub].reshape(MLA_WIDTH, tq).T
        o_ref[0, pl.ds(sub * tq, tq), :] = _rms(att, g_ref[...]).astype(BF16)


def _attn_call(q, segs, g_att, l, tq_blk, tq):
    B, _, S, _ = q.shape
    nseg = len(segs)
    ttot = sum(k.shape[2] for k, _ in segs)
    in_specs = [pl.BlockSpec((1, HEADS, tq_blk, HEAD_PAD), lambda b, t: (b, 0, t, 0))]
    args = [q]
    for k, v in segs:
        tn = k.shape[2]
        in_specs.append(pl.BlockSpec((1, HEADS, tn, HEAD_PAD), lambda b, t: (b, 0, 0, 0)))
        in_specs.append(pl.BlockSpec((1, HEADS, V_EXT, tn), lambda b, t: (b, 0, 0, 0)))
        args += [k, v]
    in_specs.append(_layer_spec((1, MLA_WIDTH), l))
    args.append(g_att)
    return pl.pallas_call(
        functools.partial(_attn_kernel, nseg=nseg, tq=tq),
        out_shape=jax.ShapeDtypeStruct((B, S, MLA_WIDTH), BF16),
        grid=(B, S // tq_blk),
        in_specs=in_specs,
        out_specs=pl.BlockSpec((1, tq_blk, MLA_WIDTH), lambda b, t: (b, t, 0)),
        scratch_shapes=[pltpu.VMEM((ttot, tq), F32), pltpu.VMEM((ttot, tq), F32),
                        pltpu.VMEM((tq_blk // tq, HEADS, V_DIM, tq), F32)],
        compiler_params=pltpu.CompilerParams(
            dimension_semantics=("parallel", "arbitrary"), vmem_limit_bytes=VMEM_LIMIT),
        name="attn",
    )(*args)


def _out_kernel(x_ref, att_ref, hs_ref, gr_ref, sp_ref, mod_ref, n2_ref, grec_ref, wo_ref, wfi_ref, wfo_ref,
                fn_ref, o_ref, *, final):
    mod = mod_ref[...]
    rows = x_ref.shape[1] // OUT_SPLIT
    x1s, h2s = [], []
    for r in range(OUT_SPLIT):
        sl = pl.ds(r * rows, rows)
        rec = _rms(hs_ref[0, sl, :] * jax.nn.gelu(gr_ref[0, sl, :]), grec_ref[...]).astype(BF16)
        y = _dot(att_ref[0, sl, :], wo_ref[pl.ds(0, MLA_WIDTH), :])
        y = y + _dot(rec, wo_ref[pl.ds(MLA_WIDTH, LRU_WIDTH), :])
        y = y + _dot(sp_ref[0, sl, :], wo_ref[pl.ds(MLA_WIDTH + LRU_WIDTH, SGU_WIDTH), :])
        x1 = x_ref[0, sl, :] + mod[2:3] * y
        x1s.append(x1)
        h2s.append((_rms(x1, n2_ref[...]) * (1.0 + mod[4:5]) + mod[3:4]).astype(BF16))
    for r in range(OUT_SPLIT):
        acc = None
        for c0, cw in FF_CHUNKS:
            g = _dot(h2s[r], wfi_ref[:, pl.ds(c0, cw)])
            u = _dot(h2s[r], wfi_ref[:, pl.ds(D_FF + c0, cw)])
            act = (jax.nn.silu(g) * u).astype(BF16)
            part = _dot(act, wfo_ref[pl.ds(c0, cw), :])
            acc = part if acc is None else acc + part
        x2 = x1s[r] + mod[5:6] * acc
        if final:
            x2 = _rms(x2, fn_ref[...])
        o_ref[0, pl.ds(r * rows, rows), :] = x2


def _out_call(x, att, hsum, gr, sp, mod, mod_row, lw, l, final_norm, final, tm):
    B, T, _ = x.shape
    tok = lambda w: pl.BlockSpec((1, tm, w), lambda b, t: (b, t, 0))
    W = functools.partial(_layer_spec, l=l)
    return pl.pallas_call(
        functools.partial(_out_kernel, final=final),
        out_shape=jax.ShapeDtypeStruct((B, T, D_MODEL), F32),
        grid=(B, T // tm),
        in_specs=[tok(D_MODEL), tok(MLA_WIDTH), tok(LRU_WIDTH), tok(LRU_WIDTH), tok(SGU_WIDTH),
                  _mod_spec(l, mod_row),
                  W((1, D_MODEL)),
                  W((1, LRU_WIDTH)),
                  W((D_MODEL, D_MODEL)),
                  W((D_MODEL, 2 * D_FF)),
                  W((D_FF, D_MODEL)),
                  _const_spec((1, D_MODEL))],
        out_specs=tok(D_MODEL),
        compiler_params=pltpu.CompilerParams(
            dimension_semantics=("parallel", "parallel"), vmem_limit_bytes=VMEM_LIMIT),
        name="out_ffn",
    )(x, att, hsum, gr, sp, mod, lw["norm2"], lw["g_rec"], lw["w_out"], lw["w_ffn_in"], lw["w_ffn_out"],
      final_norm)


def _rot_cols(w):
    q = ROPE // 4
    return jnp.concatenate([-w[..., q:2 * q], w[..., 0:q], -w[..., 3 * q:4 * q], w[..., 2 * q:3 * q]], axis=-1)


def _prep_params(p):
    L = DEPTH
    splits = np.cumsum([Q_LORA, KV_LORA, ROPE, LRU_WIDTH, LRU_WIDTH, SGU_WIDTH])
    qa, kva, kr, xr, gr, su, sv = jnp.split(p["w_in"].astype(BF16), splits, axis=2)
    kr_blk = jnp.concatenate([kr, _rot_cols(kr), jnp.zeros((L, D_MODEL, LANE - 2 * ROPE), BF16)], axis=2)
    w_in_ext = jnp.concatenate([qa, kva, kr_blk, xr, gr, su, sv], axis=2)

    wq = p["w_q_b"].astype(BF16).reshape(L, Q_LORA, HEADS, NOPE + ROPE)
    zpad = jnp.zeros((L, Q_LORA, HEADS, HEAD_PAD - NOPE - ROPE), BF16)
    w_q = jnp.concatenate([wq, zpad], axis=-1).reshape(L, Q_LORA, HEADS * HEAD_PAD)

    wkv = p["w_kv_b"].astype(BF16).reshape(L, KV_LORA, HEADS, NOPE + V_DIM)
    wk = jnp.concatenate([wkv[..., :NOPE], jnp.zeros((L, KV_LORA, HEADS, HEAD_PAD - NOPE), BF16)],
                         axis=-1).reshape(L, KV_LORA, HEADS * HEAD_PAD)
    wv = wkv[..., NOPE:].reshape(L, KV_LORA, MLA_WIDTH)
    place = np.zeros((LANE, HEADS, HEAD_PAD), np.float32)
    place[np.arange(ROPE), :, NOPE + np.arange(ROPE)] = 1.0
    place = jnp.broadcast_to(jnp.asarray(place.reshape(LANE, HEADS * HEAD_PAD), dtype=BF16),
                             (L, LANE, HEADS * HEAD_PAD))
    w_kv = jnp.concatenate([wk, place], axis=1)
    w_vt = jnp.swapaxes(wv, 1, 2)

    eye = np.eye(LRU_HEADS, dtype=np.float32)[None, :, None, :, None]

    def block_diag(w):
        _, hh, bi, bj = w.shape
        return (eye * w[:, :, :, None, :]).reshape(L, hh * bi, hh * bj)

    w_gates = (0.5 * jnp.concatenate([block_diag(p["lru_w_r"][:, 0]), block_diag(p["lru_w_i"][:, 0]),
                                      block_diag(p["lru_w_r"][:, 1]), block_diag(p["lru_w_i"][:, 1])],
                                     axis=2)).astype(BF16)
    b_gates = 0.5 * jnp.concatenate([p["lru_b_r"][:, 0], p["lru_b_i"][:, 0],
                                     p["lru_b_r"][:, 1], p["lru_b_i"][:, 1]], axis=1)[:, None]

    sgu_wcat = jnp.transpose(p["sgu_w"], (0, 2, 1, 3)).reshape(L, CHUNK, SGU_GROUPS * CHUNK).astype(BF16)
    sgu_bias = jnp.repeat(jnp.swapaxes(p["sgu_b"], 1, 2), SGU_GROUP_DIM, axis=2)
    gidx = np.arange(SGU_WIDTH) // SGU_GROUP_DIM
    gmat = jnp.asarray((gidx[:, None] == gidx[None, :]).astype(np.float32) / SGU_GROUP_DIM, dtype=BF16)

    on = p["out_norm"][:, None]
    row = lambda a: a[:, None]
    return {
        "norm1": row(p["norm1"]), "norm2": row(p["norm2"]),
        "w_in": w_in_ext, "q_a_norm": row(p["q_a_norm"]), "w_q": w_q,
        "kv_a_norm": row(p["kv_a_norm"]), "w_kv": w_kv, "w_vt": w_vt,
        "conv_w": p["conv_w"], "conv_b": row(p["conv_b"]),
        "w_gates": w_gates, "b_gates": b_gates, "lru_lam": p["lru_lam"],
        "sgu_wcat": sgu_wcat, "sgu_bias": sgu_bias, "sgu_norm": row(p["sgu_norm"]), "gmat": gmat,
        "g_att": on[..., :MLA_WIDTH], "g_rec": on[..., MLA_WIDTH:MLA_WIDTH + LRU_WIDTH],
        "g_sp": on[..., MLA_WIDTH + LRU_WIDTH:],
        "w_out": p["w_out"].astype(BF16), "w_ffn_in": p["w_ffn_in"].astype(BF16),
        "w_ffn_out": p["w_ffn_out"].astype(BF16),
    }


def _rope_tables(seq_len, ctx_len):
    f32 = np.float32
    rows = seq_len // GRID_W
    row = np.repeat(np.arange(rows, dtype=f32), GRID_W)
    col = np.tile(np.arange(GRID_W, dtype=f32), rows)
    half = ROPE // 2
    freq = (f32(ROPE_BASE) ** (-np.arange(0, half, 2, dtype=f32) / f32(half))).astype(f32)
    ar = row[:, None] * freq
    ac = col[:, None] * freq
    ang = np.concatenate([ar, ar, ac, ac], axis=-1).astype(f32)
    cos, sin = np.cos(ang).astype(f32), np.sin(ang).astype(f32)
    qs = f32(math.log2(math.e) / math.sqrt(NOPE + ROPE))

    up = ((np.arange(ROPE) // ROT_SHIFT) % 2 == 0).astype(f32)

    def q_tabs(c, s, n):
        pad = np.zeros((n, HEAD_PAD - NOPE - ROPE), f32)
        nope0 = np.zeros((n, NOPE), f32)
        cq = np.concatenate([np.ones((n, NOPE), f32), c, pad], axis=1)
        sa = np.concatenate([nope0, -s * up, pad], axis=1)
        sb = np.concatenate([nope0, s * (1 - up), pad], axis=1)
        tk = np.concatenate([c, s, np.zeros((n, LANE - 2 * ROPE), f32)], axis=1)
        return jnp.asarray(cq * qs), jnp.asarray(sa * qs), jnp.asarray(sb * qs), jnp.asarray(tk)

    lat = q_tabs(cos, sin, seq_len)
    ctx = q_tabs(np.ones((ctx_len, ROPE), f32), np.zeros((ctx_len, ROPE), f32), ctx_len)
    return lat, ctx


def kernel(x, c, ctx, c_ctx, norm1, norm2, w_ada, b_ada, w_in, q_a_norm, w_q_b, kv_a_norm, w_kv_b, conv_w, conv_b, lru_w_r, lru_b_r, lru_w_i, lru_b_i, lru_lam, sgu_norm, sgu_w, sgu_b, out_norm, w_out, w_ffn_in, w_ffn_out, final_norm):
    p = dict(norm1=norm1, norm2=norm2, w_in=w_in, q_a_norm=q_a_norm, w_q_b=w_q_b, kv_a_norm=kv_a_norm,
             w_kv_b=w_kv_b, conv_w=conv_w, conv_b=conv_b, lru_w_r=lru_w_r, lru_b_r=lru_b_r,
             lru_w_i=lru_w_i, lru_b_i=lru_b_i, lru_lam=lru_lam, sgu_norm=sgu_norm, sgu_w=sgu_w,
             sgu_b=sgu_b, out_norm=out_norm, w_out=w_out, w_ffn_in=w_ffn_in, w_ffn_out=w_ffn_out)
    B, S, _ = x.shape
    tctx = ctx.shape[1]
    mod_rows = ((B + 1 + SUBLANE - 1) // SUBLANE) * SUBLANE
    cc = jnp.concatenate([c, c_ctx[None], jnp.zeros((mod_rows - B - 1, D_MODEL), F32)], axis=0)
    mod = _ada_call(cc, w_ada, b_ada).reshape(DEPTH, mod_rows, 6, D_MODEL)
    tabs_l, tabs_c = _rope_tables(S, tctx)
    fn = final_norm[None]
    lw = _prep_params(p)

    h_ctx = ctx
    for l in range(DEPTH):
        last = l == DEPTH - 1
        q_l, k_l, v_l, xr_l, gr_l, sp_l = _inproj_call(x, mod, None, tabs_l, lw, l, 512)
        q_c, k_c, v_c, xr_c, gr_c, sp_c = _inproj_call(h_ctx, mod, B, tabs_c, lw, l, 256)
        hs_c, hs_l = _lru_call(xr_c, xr_l, lw, l)
        att_l = _attn_call(q_l, [(k_c, v_c), (k_l, v_l)], lw["g_att"], l, 1024, 256)
        x = _out_call(x, att_l, hs_l, gr_l, sp_l, mod, None, lw, l, fn, last, 512)
        if not last:
            att_c = _attn_call(q_c, [(k_c, v_c)], lw["g_att"], l, 256, 256)
            h_ctx = _out_call(h_ctx, att_c, hs_c, gr_c, sp_c, mod, B, lw, l, fn, False, 256)
    return x
```

```python
import functools
import math

import jax
import jax.numpy as jnp
import numpy as np
from jax import lax
from jax.experimental import pallas as pl
from jax.experimental.pallas import tpu as pltpu

F32 = jnp.float32
BF16 = jnp.bfloat16

D_MODEL = 1024
DEPTH = 2
GRID_W = 64
EPS = 1e-6
ROPE_BASE = 10000.0
HEADS = 8
NOPE = 64
ROPE = 32
V_DIM = 64
Q_LORA = 256
KV_LORA = 128
LRU_WIDTH = 256
LRU_HEADS = 4
CONV_W = 4
LRU_C = 8.0
SGU_GROUPS = 4
SGU_WIDTH = 256
SGU_GROUP_DIM = SGU_WIDTH // SGU_GROUPS
CHUNK = 128
D_FF = 2816
MLA_WIDTH = HEADS * V_DIM
V_EXT = V_DIM + 16

LANE = 128
SUBLANE = 8
HEAD_PAD = LANE
ROT_SHIFT = ROPE // 4
VMEM_LIMIT = 56 * 1024 * 1024

IN_EXT = 1536
C_QA, C_KVA, C_KR, C_XR, C_GR, C_SU, C_SV = 0, 256, 384, 512, 768, 1024, 1280
MXU_TILE = 256
FF_CHUNKS = ((0, 6 * MXU_TILE), (6 * MXU_TILE, 5 * MXU_TILE))
OUT_SPLIT = 2
NORM_ROWS = 128


def _rms(x, g):
    ms = jnp.mean(x * x, axis=-1, keepdims=True)
    return x * lax.rsqrt(ms + EPS) * g


def _dot(a, b):
    return jnp.dot(a, b, preferred_element_type=F32)


def _const_spec(shape):
    zeros = (0,) * len(shape)
    return pl.BlockSpec(shape, lambda *_: zeros, pipeline_mode=pl.Buffered(1))


def _layer_spec(shape, l):
    zeros = (0,) * len(shape)
    return pl.BlockSpec((None,) + tuple(shape), lambda *_: (l,) + zeros, pipeline_mode=pl.Buffered(1))


def _mod_spec(l, row):
    if row is None:
        return pl.BlockSpec((None, None, 6, D_MODEL), lambda b, t: (l, b, 0, 0))
    return pl.BlockSpec((None, None, 6, D_MODEL), lambda b, t: (l, row, 0, 0))


ADA_TILE = 512


def _ada_kernel(c_ref, w_ref, b_ref, o_ref):
    s = jax.nn.silu(c_ref[...]).astype(BF16)
    o_ref[0] = _dot(s, w_ref[0].astype(BF16)) + b_ref[0]


def _ada_call(cc, w_ada, b_ada):
    rows = cc.shape[0]
    n = w_ada.shape[-1]
    return pl.pallas_call(
        _ada_kernel,
        out_shape=jax.ShapeDtypeStruct((DEPTH, rows, n), F32),
        grid=(DEPTH, n // ADA_TILE),
        in_specs=[
            pl.BlockSpec((rows, D_MODEL), lambda l, j: (0, 0)),
            pl.BlockSpec((1, D_MODEL, ADA_TILE), lambda l, j: (l, 0, j)),
            pl.BlockSpec((1, 1, ADA_TILE), lambda l, j: (l, 0, j)),
        ],
        out_specs=pl.BlockSpec((1, rows, ADA_TILE), lambda l, j: (l, 0, j)),
        compiler_params=pltpu.CompilerParams(
            dimension_semantics=("arbitrary", "arbitrary"), vmem_limit_bytes=VMEM_LIMIT),
        name="ada_mod",
    )(cc, w_ada, b_ada.reshape(DEPTH, 1, n))


def _inproj_kernel(x_ref, mod_ref, n1_ref, win_ref, qan_ref, wq_ref, kvn_ref, wkv_ref, wvt_ref,
                   cq_ref, sa_ref, sb_ref, tk_ref, wcat_ref, sbias_ref, sgn_ref, gmat_ref, gsp_ref,
                   q_ref, k_ref, v_ref, xr_ref, gr_ref, sp_ref, *, tm):
    mod = mod_ref[...]
    zs = []
    for k in range(tm // NORM_ROWS):
        xs = x_ref[0, pl.ds(k * NORM_ROWS, NORM_ROWS), :]
        h = _rms(xs, n1_ref[...]) * (1.0 + mod[1:2]) + mod[0:1]
        zs.append(_dot(h.astype(BF16), win_ref[...]))
    z = jnp.concatenate(zs, axis=0)

    def queries():
        qn = _rms(z[:, C_QA:C_QA + Q_LORA], qan_ref[...]).astype(BF16)
        qq = _dot(qn, wq_ref[...])
        cq = cq_ref[...]
        sa = sa_ref[...]
        sb = sb_ref[...]
        for hd in range(HEADS):
            qh = qq[:, hd * HEAD_PAD:(hd + 1) * HEAD_PAD]
            qh = (qh * cq + pltpu.roll(qh, LANE - ROT_SHIFT, axis=1) * sa
                  + pltpu.roll(qh, ROT_SHIFT, axis=1) * sb)
            q_ref[0, hd] = qh.astype(BF16)

    def keys_values():
        kvn = _rms(z[:, C_KVA:C_KVA + KV_LORA], kvn_ref[...]).astype(BF16)
        t = z[:, C_KR:C_KR + LANE] * tk_ref[...]
        kro = t + pltpu.roll(t, LANE - ROPE, axis=1)
        lhs = jnp.concatenate([kvn, kro.astype(BF16)], axis=1)
        kk = _dot(lhs, wkv_ref[...])
        for hd in range(HEADS):
            k_ref[0, hd] = kk[:, hd * HEAD_PAD:(hd + 1) * HEAD_PAD].astype(BF16)
        vt = lax.dot_general(wvt_ref[...], kvn, (((1,), (1,)), ((), ())), preferred_element_type=F32)
        tail_row = lax.broadcasted_iota(jnp.int32, (V_EXT - V_DIM, tm), 0)
        tail = jnp.where(tail_row == 0, 1.0, 0.0).astype(BF16)
        for hd in range(HEADS):
            v_ref[0, hd] = jnp.concatenate([vt[hd * V_DIM:(hd + 1) * V_DIM].astype(BF16), tail], axis=0)

    def recurrence_inputs():
        xr_ref[0] = z[:, C_XR:C_XR + LRU_WIDTH]
        gr_ref[0] = z[:, C_GR:C_GR + LRU_WIDTH]

    def spatial_gating():
        u = jax.nn.gelu(z[:, C_SU:C_SU + SGU_WIDTH])
        vg = jax.nn.gelu(z[:, C_SV:C_SV + SGU_WIDTH])
        v2 = vg * vg
        v2_hi = v2.astype(BF16)
        v2_lo = (v2 - v2_hi.astype(F32)).astype(BF16)
        gmat = gmat_ref[...]
        gms = _dot(v2_hi, gmat) + _dot(v2_lo, gmat)
        vb = (vg * lax.rsqrt(gms + EPS) * sgn_ref[...]).astype(BF16)
        grp = lax.broadcasted_iota(jnp.int32, (CHUNK, SGU_WIDTH), 1) // SGU_GROUP_DIM
        zero = jnp.zeros((CHUNK, SGU_WIDTH), BF16)
        wcat = wcat_ref[...]
        sbias = sbias_ref[...]
        parts = []
        for c in range(tm // CHUNK):
            vc = vb[c * CHUNK:(c + 1) * CHUNK]
            rhs = jnp.concatenate([jnp.where(grp == g, vc, zero) for g in range(SGU_GROUPS)], axis=0)
            s = _dot(wcat, rhs) + sbias
            parts.append(u[c * CHUNK:(c + 1) * CHUNK] * s)
        sp = jnp.concatenate(parts, axis=0) if len(parts) > 1 else parts[0]
        sp_ref[0] = _rms(sp, gsp_ref[...]).astype(BF16)

    queries()
    spatial_gating()
    keys_values()
    recurrence_inputs()


def _inproj_call(x, mod, mod_row, tabs, lw, l, tm):
    B, T, _ = x.shape
    cq, sa, sb, tk = tabs
    tab_spec = pl.BlockSpec((tm, LANE), lambda b, t: (t, 0))
    W = functools.partial(_layer_spec, l=l)
    in_specs = [
        pl.BlockSpec((1, tm, D_MODEL), lambda b, t: (b, t, 0)),
        _mod_spec(l, mod_row),
        W((1, D_MODEL)),
        W((D_MODEL, IN_EXT)),
        W((1, Q_LORA)),
        W((Q_LORA, HEADS * HEAD_PAD)),
        W((1, KV_LORA)),
        W((2 * LANE, HEADS * HEAD_PAD)),
        W((MLA_WIDTH, KV_LORA)),
        tab_spec, tab_spec, tab_spec, tab_spec,
        W((CHUNK, SGU_GROUPS * CHUNK)),
        W((CHUNK, SGU_WIDTH)),
        W((1, SGU_WIDTH)),
        _const_spec((SGU_WIDTH, SGU_WIDTH)),
        W((1, SGU_WIDTH)),
    ]
    out_shape = [
        jax.ShapeDtypeStruct((B, HEADS, T, HEAD_PAD), BF16),
        jax.ShapeDtypeStruct((B, HEADS, T, HEAD_PAD), BF16),
        jax.ShapeDtypeStruct((B, HEADS, V_EXT, T), BF16),
        jax.ShapeDtypeStruct((B, T, LRU_WIDTH), F32),
        jax.ShapeDtypeStruct((B, T, LRU_WIDTH), F32),
        jax.ShapeDtypeStruct((B, T, SGU_WIDTH), BF16),
    ]
    out_specs = [
        pl.BlockSpec((1, HEADS, tm, HEAD_PAD), lambda b, t: (b, 0, t, 0)),
        pl.BlockSpec((1, HEADS, tm, HEAD_PAD), lambda b, t: (b, 0, t, 0)),
        pl.BlockSpec((1, HEADS, V_EXT, tm), lambda b, t: (b, 0, 0, t)),
        pl.BlockSpec((1, tm, LRU_WIDTH), lambda b, t: (b, t, 0)),
        pl.BlockSpec((1, tm, LRU_WIDTH), lambda b, t: (b, t, 0)),
        pl.BlockSpec((1, tm, SGU_WIDTH), lambda b, t: (b, t, 0)),
    ]
    return pl.pallas_call(
        functools.partial(_inproj_kernel, tm=tm),
        out_shape=out_shape,
        grid=(B, T // tm),
        in_specs=in_specs,
        out_specs=out_specs,
        compiler_params=pltpu.CompilerParams(
            dimension_semantics=("parallel", "parallel"), vmem_limit_bytes=VMEM_LIMIT),
        name="inproj",
    )(x, mod, lw["norm1"], lw["w_in"], lw["q_a_norm"], lw["w_q"], lw["kv_a_norm"], lw["w_kv"], lw["w_vt"],
      cq, sa, sb, tk, lw["sgu_wcat"], lw["sgu_bias"], lw["sgu_norm"], lw["gmat"], lw["g_sp"])


GATE_ROWS = 256
SCAN_UNROLL = 8


def _tile_scan(a, b, h_prev, row, reverse):
    for k in (1, 2, 4):
        if reverse:
            keep = row < SUBLANE - k
            sh = SUBLANE - k
        else:
            keep = row >= k
            sh = k
        a_s = jnp.where(keep, pltpu.roll(a, sh, axis=0), 1.0)
        b_s = jnp.where(keep, pltpu.roll(b, sh, axis=0), 0.0)
        b = a * b_s + b
        a = a * a_s
    h = a * h_prev + b
    last = h[0:1] if reverse else h[SUBLANE - 1:SUBLANE]
    return h, jnp.broadcast_to(last, h.shape)


def _lru_kernel(xrc_ref, xrl_ref, cw_ref, cb_ref, wg_ref, bg_ref, lam_ref,
                outc_ref, outl_ref, xc_ref, a_ref, b_ref, hs_ref, *, tc, tl):
    cw = cw_ref[...]
    cb = cb_ref[...]
    ttot = tc + tl

    def conv(src_ref, n, dst0):
        x = src_ref[0]
        row = lax.broadcasted_iota(jnp.int32, (n, LRU_WIDTH), 0)
        y = cb
        for j in range(CONV_W):
            off = j - CONV_W // 2
            if off == 0:
                tap = x
            else:
                inside = row >= -off if off < 0 else row < n - off
                tap = jnp.where(inside, pltpu.roll(x, (-off) % n, axis=0), 0.0)
            y = y + cw[j:j + 1] * tap
        xc_ref[pl.ds(dst0, n), :] = y

    conv(xrc_ref, tc, 0)
    conv(xrl_ref, tl, tc)

    half_nsp = -0.5 * LRU_C * jax.nn.softplus(-lam_ref[...])
    wg = wg_ref[...]
    bg = bg_ref[...]
    tiles_per_chunk = GATE_ROWS // SUBLANE
    for c in range(ttot // GATE_ROWS):
        xcc = xc_ref[pl.ds(c * GATE_ROWS, GATE_ROWS), :]
        g = _dot(xcc.astype(BF16), wg) + bg
        for d in range(2):
            tr = jnp.tanh(g[:, (2 * d) * LRU_WIDTH:(2 * d + 1) * LRU_WIDTH])
            i = 0.5 * jnp.tanh(g[:, (2 * d + 1) * LRU_WIDTH:(2 * d + 2) * LRU_WIDTH]) + 0.5
            log_a = half_nsp[d:d + 1] * tr + half_nsp[d:d + 1]
            a = jnp.exp(log_a)
            bb = jnp.sqrt(-jnp.tanh(log_a) * (a * a + 1.0)) * (i * xcc)
            a_ref[d, pl.ds(c * tiles_per_chunk, tiles_per_chunk)] = a.reshape(
                tiles_per_chunk, SUBLANE, LRU_WIDTH)
            b_ref[d, pl.ds(c * tiles_per_chunk, tiles_per_chunk)] = bb.reshape(
                tiles_per_chunk, SUBLANE, LRU_WIDTH)

    row = lax.broadcasted_iota(jnp.int32, (SUBLANE, LRU_WIDTH), 0)
    nc = tc // SUBLANE
    nl = tl // SUBLANE

    def run(first_f, first_r, n, hf, hr):
        def body(i, carry):
            hf, hr = carry
            for j in range(SCAN_UNROLL):
                tf = first_f + i * SCAN_UNROLL + j
                tr = first_r - i * SCAN_UNROLL - j
                of, hf = _tile_scan(a_ref[0, tf], b_ref[0, tf], hf, row, False)
                orv, hr = _tile_scan(a_ref[1, tr], b_ref[1, tr], hr, row, True)
                hs_ref[0, tf] = of
                hs_ref[1, tr] = orv
            return hf, hr
        return lax.fori_loop(0, n // SCAN_UNROLL, body, (hf, hr))

    zero = jnp.zeros((SUBLANE, LRU_WIDTH), F32)
    hf, hr = run(0, nc - 1, nc, zero, zero)
    run(nc, nc + nl - 1, nl, hf, hr)

    for c in range(ttot // GATE_ROWS):
        sl = pl.ds(c * tiles_per_chunk, tiles_per_chunk)
        hsum = (hs_ref[0, sl] + hs_ref[1, sl]).reshape(GATE_ROWS, LRU_WIDTH)
        r0 = c * GATE_ROWS
        if r0 < tc:
            outc_ref[0, pl.ds(r0, GATE_ROWS), :] = hsum
        else:
            outl_ref[0, pl.ds(r0 - tc, GATE_ROWS), :] = hsum


def _lru_call(xr_c, xr_l, lw, l):
    B, tc, _ = xr_c.shape
    tl = xr_l.shape[1]
    ttot = tc + tl
    assert tc % GATE_ROWS == 0 and tl % GATE_ROWS == 0
    seq = lambda n: pl.BlockSpec((1, n, LRU_WIDTH), lambda b: (b, 0, 0))
    W = functools.partial(_layer_spec, l=l)
    return pl.pallas_call(
        functools.partial(_lru_kernel, tc=tc, tl=tl),
        out_shape=[jax.ShapeDtypeStruct((B, tc, LRU_WIDTH), F32),
                   jax.ShapeDtypeStruct((B, tl, LRU_WIDTH), F32)],
        grid=(B,),
        in_specs=[seq(tc), seq(tl),
                  W((CONV_W, LRU_WIDTH)), W((1, LRU_WIDTH)),
                  W((LRU_WIDTH, 4 * LRU_WIDTH)), W((1, 4 * LRU_WIDTH)),
                  W((2, LRU_WIDTH))],
        out_specs=[seq(tc), seq(tl)],
        scratch_shapes=[
            pltpu.VMEM((ttot, LRU_WIDTH), F32),
            pltpu.VMEM((2, ttot // SUBLANE, SUBLANE, LRU_WIDTH), F32),
            pltpu.VMEM((2, ttot // SUBLANE, SUBLANE, LRU_WIDTH), F32),
            pltpu.VMEM((2, ttot // SUBLANE, SUBLANE, LRU_WIDTH), F32),
        ],
        compiler_params=pltpu.CompilerParams(
            dimension_semantics=("parallel",), vmem_limit_bytes=VMEM_LIMIT),
        name="lru",
    )(xr_c, xr_l, lw["conv_w"], lw["conv_b"], lw["w_gates"], lw["b_gates"], lw["lru_lam"])


RED_ROWS = 64
ATTN_UNROLL = 8


def _col_reduce(x, op):
    rows, cols = x.shape
    part = op(x.reshape(rows // RED_ROWS, RED_ROWS, cols), axis=0)
    return op(part, axis=0, keepdims=True)


def _attn_kernel(*refs, nseg, tq):
    q_ref = refs[0]
    kv_refs = refs[1:1 + 2 * nseg]
    g_ref = refs[1 + 2 * nseg]
    o_ref = refs[2 + 2 * nseg]
    s_scr = refs[3 + 2 * nseg:5 + 2 * nseg]
    o_scr = refs[5 + 2 * nseg]
    seg_len = [kv_refs[2 * s].shape[2] for s in range(nseg)]
    seg_off = [sum(seg_len[:s]) for s in range(nseg)]

    nsub = q_ref.shape[2] // tq

    def scores_into(sub, hd, slot):
        q = q_ref[0, hd, pl.ds(pl.multiple_of(sub * tq, tq), tq), :]
        m = None
        for s in reversed(range(nseg)):
            sc = lax.dot_general(kv_refs[2 * s][0, hd], q, (((1,), (1,)), ((), ())),
                                 preferred_element_type=F32)
            s_scr[slot][pl.ds(seg_off[s], seg_len[s]), :] = sc
            ms = _col_reduce(sc, jnp.max)
            m = ms if m is None else jnp.maximum(m, ms)
        return m

    def softmax_pv(sub, hd, slot, m):
        o = None
        for s in reversed(range(nseg)):
            rows = pl.ds(seg_off[s], seg_len[s])
            p = jnp.exp2(s_scr[slot][rows, :] - m).astype(BF16)
            po = _dot(kv_refs[2 * s + 1][0, hd], p)
            o = po if o is None else o + po
        o_scr[sub, hd] = o[:V_DIM] * (1.0 / o[V_DIM:V_DIM + 1])

    def run_units(u0, n_units, m, has_next):
        for j in range(n_units):
            u = u0 + j
            if j + 1 < n_units or has_next:
                m_next = scores_into((u + 1) // HEADS, (u + 1) % HEADS, (j + 1) % 2)
            else:
                m_next = None
            softmax_pv(u // HEADS, u % HEADS, j % 2, m)
            m = m_next
        return m

    n_units = nsub * HEADS
    m_a = lax.fori_loop(0, n_units // ATTN_UNROLL - 1,
                        lambda i, m: run_units(i * ATTN_UNROLL, ATTN_UNROLL, m, True),
                        scores_into(0, 0, 0))
    run_units(n_units - ATTN_UNROLL, ATTN_UNROLL, m_a, False)
    for sub in range(nsub):
        att = o_scr[sub].reshape(MLA_WIDTH, tq).T
        o_ref[0, pl.ds(sub * tq, tq), :] = _rms(att, g_ref[...]).astype(BF16)


def _attn_call(q, segs, g_att, l, tq_blk, tq):
    B, _, S, _ = q.shape
    ttot = sum(k.shape[2] for k, _, _ in segs)
    in_specs = [pl.BlockSpec((1, HEADS, tq_blk, HEAD_PAD), lambda b, t: (b, 0, t, 0))]
    args = [q]
    for k, v, parts in segs:
        tn = k.shape[2] // parts
        for j in range(parts):
            in_specs.append(pl.BlockSpec((1, HEADS, tn, HEAD_PAD), lambda b, t, j=j: (b, 0, j, 0)))
            in_specs.append(pl.BlockSpec((1, HEADS, V_EXT, tn), lambda b, t, j=j: (b, 0, 0, j)))
            args += [k, v]
    nseg = (len(args) - 1) // 2
    in_specs.append(_layer_spec((1, MLA_WIDTH), l))
    args.append(g_att)
    return pl.pallas_call(
        functools.partial(_attn_kernel, nseg=nseg, tq=tq),
        out_shape=jax.ShapeDtypeStruct((B, S, MLA_WIDTH), BF16),
        grid=(B, S // tq_blk),
        in_specs=in_specs,
        out_specs=pl.BlockSpec((1, tq_blk, MLA_WIDTH), lambda b, t: (b, t, 0)),
        scratch_shapes=[pltpu.VMEM((ttot, tq), F32), pltpu.VMEM((ttot, tq), F32),
                        pltpu.VMEM((tq_blk // tq, HEADS, V_DIM, tq), F32)],
        compiler_params=pltpu.CompilerParams(
            dimension_semantics=("parallel", "arbitrary"), vmem_limit_bytes=VMEM_LIMIT),
        name="attn",
    )(*args)


def _out_kernel(x_ref, att_ref, hs_ref, gr_ref, sp_ref, mod_ref, n2_ref, grec_ref, wo_ref, wfi_ref, wfo_ref,
                fn_ref, o_ref, *, final):
    mod = mod_ref[...]
    rows = x_ref.shape[1] // OUT_SPLIT
    x1s, h2s = [], []
    for r in range(OUT_SPLIT):
        sl = pl.ds(r * rows, rows)
        rec = _rms(hs_ref[0, sl, :] * jax.nn.gelu(gr_ref[0, sl, :]), grec_ref[...]).astype(BF16)
        y = _dot(att_ref[0, sl, :], wo_ref[pl.ds(0, MLA_WIDTH), :])
        y = y + _dot(rec, wo_ref[pl.ds(MLA_WIDTH, LRU_WIDTH), :])
        y = y + _dot(sp_ref[0, sl, :], wo_ref[pl.ds(MLA_WIDTH + LRU_WIDTH, SGU_WIDTH), :])
        x1 = x_ref[0, sl, :] + mod[2:3] * y
        x1s.append(x1)
        h2s.append((_rms(x1, n2_ref[...]) * (1.0 + mod[4:5]) + mod[3:4]).astype(BF16))
    for r in range(OUT_SPLIT):
        acc = None
        for c0, cw in FF_CHUNKS:
            g = _dot(h2s[r], wfi_ref[:, pl.ds(c0, cw)])
            u = _dot(h2s[r], wfi_ref[:, pl.ds(D_FF + c0, cw)])
            act = (jax.nn.silu(g) * u).astype(BF16)
            part = _dot(act, wfo_ref[pl.ds(c0, cw), :])
            acc = part if acc is None else acc + part
        x2 = x1s[r] + mod[5:6] * acc
        if final:
            x2 = _rms(x2, fn_ref[...])
        o_ref[0, pl.ds(r * rows, rows), :] = x2


def _out_call(x, att, hsum, gr, sp, mod, mod_row, lw, l, final_norm, final, tm):
    B, T, _ = x.shape
    tok = lambda w: pl.BlockSpec((1, tm, w), lambda b, t: (b, t, 0))
    W = functools.partial(_layer_spec, l=l)
    return pl.pallas_call(
        functools.partial(_out_kernel, final=final),
        out_shape=jax.ShapeDtypeStruct((B, T, D_MODEL), F32),
        grid=(B, T // tm),
        in_specs=[tok(D_MODEL), tok(MLA_WIDTH), tok(LRU_WIDTH), tok(LRU_WIDTH), tok(SGU_WIDTH),
                  _mod_spec(l, mod_row),
                  W((1, D_MODEL)),
                  W((1, LRU_WIDTH)),
                  W((D_MODEL, D_MODEL)),
                  W((D_MODEL, 2 * D_FF)),
                  W((D_FF, D_MODEL)),
                  _const_spec((1, D_MODEL))],
        out_specs=tok(D_MODEL),
        compiler_params=pltpu.CompilerParams(
            dimension_semantics=("parallel", "parallel"), vmem_limit_bytes=VMEM_LIMIT),
        name="out_ffn",
    )(x, att, hsum, gr, sp, mod, lw["norm2"], lw["g_rec"], lw["w_out"], lw["w_ffn_in"], lw["w_ffn_out"],
      final_norm)


def _rot_cols(w):
    q = ROPE // 4
    return jnp.concatenate([-w[..., q:2 * q], w[..., 0:q], -w[..., 3 * q:4 * q], w[..., 2 * q:3 * q]], axis=-1)


def _prep_params(p):
    L = DEPTH
    splits = np.cumsum([Q_LORA, KV_LORA, ROPE, LRU_WIDTH, LRU_WIDTH, SGU_WIDTH])
    qa, kva, kr, xr, gr, su, sv = jnp.split(p["w_in"].astype(BF16), splits, axis=2)
    kr_blk = jnp.concatenate([kr, _rot_cols(kr), jnp.zeros((L, D_MODEL, LANE - 2 * ROPE), BF16)], axis=2)
    w_in_ext = jnp.concatenate([qa, kva, kr_blk, xr, gr, su, sv], axis=2)

    wq = p["w_q_b"].astype(BF16).reshape(L, Q_LORA, HEADS, NOPE + ROPE)
    zpad = jnp.zeros((L, Q_LORA, HEADS, HEAD_PAD - NOPE - ROPE), BF16)
    w_q = jnp.concatenate([wq, zpad], axis=-1).reshape(L, Q_LORA, HEADS * HEAD_PAD)

    wkv = p["w_kv_b"].astype(BF16).reshape(L, KV_LORA, HEADS, NOPE + V_DIM)
    wk = jnp.concatenate([wkv[..., :NOPE], jnp.zeros((L, KV_LORA, HEADS, HEAD_PAD - NOPE), BF16)],
                         axis=-1).reshape(L, KV_LORA, HEADS * HEAD_PAD)
    wv = wkv[..., NOPE:].reshape(L, KV_LORA, MLA_WIDTH)
    place = np.zeros((LANE, HEADS, HEAD_PAD), np.float32)
    place[np.arange(ROPE), :, NOPE + np.arange(ROPE)] = 1.0
    place = jnp.broadcast_to(jnp.asarray(place.reshape(LANE, HEADS * HEAD_PAD), dtype=BF16),
                             (L, LANE, HEADS * HEAD_PAD))
    w_kv = jnp.concatenate([wk, place], axis=1)
    w_vt = jnp.swapaxes(wv, 1, 2)

    eye = np.eye(LRU_HEADS, dtype=np.float32)[None, :, None, :, None]

    def block_diag(w):
        _, hh, bi, bj = w.shape
        return (eye * w[:, :, :, None, :]).reshape(L, hh * bi, hh * bj)

    w_gates = (0.5 * jnp.concatenate([block_diag(p["lru_w_r"][:, 0]), block_diag(p["lru_w_i"][:, 0]),
                                      block_diag(p["lru_w_r"][:, 1]), block_diag(p["lru_w_i"][:, 1])],
                                     axis=2)).astype(BF16)
    b_gates = 0.5 * jnp.concatenate([p["lru_b_r"][:, 0], p["lru_b_i"][:, 0],
                                     p["lru_b_r"][:, 1], p["lru_b_i"][:, 1]], axis=1)[:, None]

    sgu_wcat = jnp.transpose(p["sgu_w"], (0, 2, 1, 3)).reshape(L, CHUNK, SGU_GROUPS * CHUNK).astype(BF16)
    sgu_bias = jnp.repeat(jnp.swapaxes(p["sgu_b"], 1, 2), SGU_GROUP_DIM, axis=2)
    gidx = np.arange(SGU_WIDTH) // SGU_GROUP_DIM
    gmat = jnp.asarray((gidx[:, None] == gidx[None, :]).astype(np.float32) / SGU_GROUP_DIM, dtype=BF16)

    on = p["out_norm"][:, None]
    row = lambda a: a[:, None]
    return {
        "norm1": row(p["norm1"]), "norm2": row(p["norm2"]),
        "w_in": w_in_ext, "q_a_norm": row(p["q_a_norm"]), "w_q": w_q,
        "kv_a_norm": row(p["kv_a_norm"]), "w_kv": w_kv, "w_vt": w_vt,
        "conv_w": p["conv_w"], "conv_b": row(p["conv_b"]),
        "w_gates": w_gates, "b_gates": b_gates, "lru_lam": p["lru_lam"],
        "sgu_wcat": sgu_wcat, "sgu_bias": sgu_bias, "sgu_norm": row(p["sgu_norm"]), "gmat": gmat,
        "g_att": on[..., :MLA_WIDTH], "g_rec": on[..., MLA_WIDTH:MLA_WIDTH + LRU_WIDTH],
        "g_sp": on[..., MLA_WIDTH + LRU_WIDTH:],
        "w_out": p["w_out"].astype(BF16), "w_ffn_in": p["w_ffn_in"].astype(BF16),
        "w_ffn_out": p["w_ffn_out"].astype(BF16),
    }


def _rope_tables(seq_len, ctx_len):
    f32 = np.float32
    rows = seq_len // GRID_W
    row = np.repeat(np.arange(rows, dtype=f32), GRID_W)
    col = np.tile(np.arange(GRID_W, dtype=f32), rows)
    half = ROPE // 2
    freq = (f32(ROPE_BASE) ** (-np.arange(0, half, 2, dtype=f32) / f32(half))).astype(f32)
    ar = row[:, None] * freq
    ac = col[:, None] * freq
    ang = np.concatenate([ar, ar, ac, ac], axis=-1).astype(f32)
    cos, sin = np.cos(ang).astype(f32), np.sin(ang).astype(f32)
    qs = f32(math.log2(math.e) / math.sqrt(NOPE + ROPE))

    up = ((np.arange(ROPE) // ROT_SHIFT) % 2 == 0).astype(f32)

    def q_tabs(c, s, n):
        pad = np.zeros((n, HEAD_PAD - NOPE - ROPE), f32)
        nope0 = np.zeros((n, NOPE), f32)
        cq = np.concatenate([np.ones((n, NOPE), f32), c, pad], axis=1)
        sa = np.concatenate([nope0, -s * up, pad], axis=1)
        sb = np.concatenate([nope0, s * (1 - up), pad], axis=1)
        tk = np.concatenate([c, s, np.zeros((n, LANE - 2 * ROPE), f32)], axis=1)
        return jnp.asarray(cq * qs), jnp.asarray(sa * qs), jnp.asarray(sb * qs), jnp.asarray(tk)

    lat = q_tabs(cos, sin, seq_len)
    ctx = q_tabs(np.ones((ctx_len, ROPE), f32), np.zeros((ctx_len, ROPE), f32), ctx_len)
    return lat, ctx


def kernel(x, c, ctx, c_ctx, norm1, norm2, w_ada, b_ada, w_in, q_a_norm, w_q_b, kv_a_norm, w_kv_b, conv_w, conv_b, lru_w_r, lru_b_r, lru_w_i, lru_b_i, lru_lam, sgu_norm, sgu_w, sgu_b, out_norm, w_out, w_ffn_in, w_ffn_out, final_norm):
    p = dict(norm1=norm1, norm2=norm2, w_in=w_in, q_a_norm=q_a_norm, w_q_b=w_q_b, kv_a_norm=kv_a_norm,
             w_kv_b=w_kv_b, conv_w=conv_w, conv_b=conv_b, lru_w_r=lru_w_r, lru_b_r=lru_b_r,
             lru_w_i=lru_w_i, lru_b_i=lru_b_i, lru_lam=lru_lam, sgu_norm=sgu_norm, sgu_w=sgu_w,
             sgu_b=sgu_b, out_norm=out_norm, w_out=w_out, w_ffn_in=w_ffn_in, w_ffn_out=w_ffn_out)
    B, S, _ = x.shape
    tctx = ctx.shape[1]
    mod_rows = ((B + 1 + SUBLANE - 1) // SUBLANE) * SUBLANE
    cc = jnp.concatenate([c, c_ctx[None], jnp.zeros((mod_rows - B - 1, D_MODEL), F32)], axis=0)
    mod = _ada_call(cc, w_ada, b_ada).reshape(DEPTH, mod_rows, 6, D_MODEL)
    tabs_l, tabs_c = _rope_tables(S, tctx)
    fn = final_norm[None]
    lw = _prep_params(p)

    h_ctx = ctx
    for l in range(DEPTH):
        last = l == DEPTH - 1
        q_l, k_l, v_l, xr_l, gr_l, sp_l = _inproj_call(x, mod, None, tabs_l, lw, l, 512)
        q_c, k_c, v_c, xr_c, gr_c, sp_c = _inproj_call(h_ctx, mod, B, tabs_c, lw, l, 256)
        hs_c, hs_l = _lru_call(xr_c, xr_l, lw, l)
        att_l = _attn_call(q_l, [(k_c, v_c, 1), (k_l, v_l, 2)], lw["g_att"], l, 1024, 256)
        x = _out_call(x, att_l, hs_l, gr_l, sp_l, mod, None, lw, l, fn, last, 512)
        if not last:
            att_c = _attn_call(q_c, [(k_c, v_c, 1)], lw["g_att"], l, 256, 256)
            h_ctx = _out_call(h_ctx, att_c, hs_c, gr_c, sp_c, mod, B, lw, l, fn, False, 256)
    return x
```

```python
import functools
import math

import jax
import jax.numpy as jnp
import numpy as np
from jax import lax
from jax.experimental import pallas as pl
from jax.experimental.pallas import tpu as pltpu

F32 = jnp.float32
BF16 = jnp.bfloat16

D_MODEL = 1024
DEPTH = 2
GRID_W = 64
EPS = 1e-6
ROPE_BASE = 10000.0
HEADS = 8
NOPE = 64
ROPE = 32
V_DIM = 64
Q_LORA = 256
KV_LORA = 128
LRU_WIDTH = 256
LRU_HEADS = 4
CONV_W = 4
LRU_C = 8.0
SGU_GROUPS = 4
SGU_WIDTH = 256
SGU_GROUP_DIM = SGU_WIDTH // SGU_GROUPS
CHUNK = 128
D_FF = 2816
MLA_WIDTH = HEADS * V_DIM
V_EXT = V_DIM + 16

LANE = 128
SUBLANE = 8
HEAD_PAD = LANE
ROT_SHIFT = ROPE // 4
VMEM_LIMIT = 56 * 1024 * 1024

IN_EXT = 1536
C_QA, C_KVA, C_KR, C_XR, C_GR, C_SU, C_SV = 0, 256, 384, 512, 768, 1024, 1280
MXU_TILE = 256
FF_CHUNKS = ((0, 6 * MXU_TILE), (6 * MXU_TILE, 5 * MXU_TILE))
OUT_SPLIT = 2
NORM_ROWS = 128


def _rms(x, g):
    ms = jnp.mean(x * x, axis=-1, keepdims=True)
    return x * lax.rsqrt(ms + EPS) * g


def _dot(a, b):
    return jnp.dot(a, b, preferred_element_type=F32)


def _const_spec(shape):
    zeros = (0,) * len(shape)
    return pl.BlockSpec(shape, lambda *_: zeros, pipeline_mode=pl.Buffered(1))


def _layer_spec(shape, l):
    zeros = (0,) * len(shape)
    return pl.BlockSpec((None,) + tuple(shape), lambda *_: (l,) + zeros, pipeline_mode=pl.Buffered(1))


def _mod_spec(l, row):
    if row is None:
        return pl.BlockSpec((None, None, 6, D_MODEL), lambda b, t: (l, b, 0, 0))
    return pl.BlockSpec((None, None, 6, D_MODEL), lambda b, t: (l, row, 0, 0))


ADA_TILE = 512


def _ada_kernel(c_ref, w_ref, b_ref, o_ref):
    s = jax.nn.silu(c_ref[...]).astype(BF16)
    o_ref[0] = _dot(s, w_ref[0].astype(BF16)) + b_ref[0]


def _ada_call(cc, w_ada, b_ada):
    rows = cc.shape[0]
    n = w_ada.shape[-1]
    return pl.pallas_call(
        _ada_kernel,
        out_shape=jax.ShapeDtypeStruct((DEPTH, rows, n), F32),
        grid=(DEPTH, n // ADA_TILE),
        in_specs=[
            pl.BlockSpec((rows, D_MODEL), lambda l, j: (0, 0)),
            pl.BlockSpec((1, D_MODEL, ADA_TILE), lambda l, j: (l, 0, j)),
            pl.BlockSpec((1, 1, ADA_TILE), lambda l, j: (l, 0, j)),
        ],
        out_specs=pl.BlockSpec((1, rows, ADA_TILE), lambda l, j: (l, 0, j)),
        compiler_params=pltpu.CompilerParams(
            dimension_semantics=("arbitrary", "arbitrary"), vmem_limit_bytes=VMEM_LIMIT),
        name="ada_mod",
    )(cc, w_ada, b_ada.reshape(DEPTH, 1, n))


def _inproj_kernel(x_ref, mod_ref, n1_ref, win_ref, qan_ref, wq_ref, kvn_ref, wkv_ref, wvt_ref,
                   cq_ref, sa_ref, sb_ref, tk_ref, wcat_ref, sbias_ref, sgn_ref, gmat_ref, gsp_ref,
                   q_ref, k_ref, v_ref, xr_ref, gr_ref, sp_ref, *, tm):
    mod = mod_ref[...]
    zs = []
    for k in range(tm // NORM_ROWS):
        xs = x_ref[0, pl.ds(k * NORM_ROWS, NORM_ROWS), :]
        h = _rms(xs, n1_ref[...]) * (1.0 + mod[1:2]) + mod[0:1]
        zs.append(_dot(h.astype(BF16), win_ref[...]))
    z = jnp.concatenate(zs, axis=0)

    def queries():
        qn = _rms(z[:, C_QA:C_QA + Q_LORA], qan_ref[...]).astype(BF16)
        qq = _dot(qn, wq_ref[...])
        cq = cq_ref[...]
        sa = sa_ref[...]
        sb = sb_ref[...]
        for hd in range(HEADS):
            qh = qq[:, hd * HEAD_PAD:(hd + 1) * HEAD_PAD]
            qh = (qh * cq + pltpu.roll(qh, LANE - ROT_SHIFT, axis=1) * sa
                  + pltpu.roll(qh, ROT_SHIFT, axis=1) * sb)
            q_ref[0, hd] = qh.astype(BF16)

    def keys_values():
        kvn = _rms(z[:, C_KVA:C_KVA + KV_LORA], kvn_ref[...]).astype(BF16)
        t = z[:, C_KR:C_KR + LANE] * tk_ref[...]
        kro = t + pltpu.roll(t, LANE - ROPE, axis=1)
        lhs = jnp.concatenate([kvn, kro.astype(BF16)], axis=1)
        kk = _dot(lhs, wkv_ref[...])
        for hd in range(HEADS):
            k_ref[0, hd] = kk[:, hd * HEAD_PAD:(hd + 1) * HEAD_PAD].astype(BF16)
        vt = lax.dot_general(wvt_ref[...], kvn, (((1,), (1,)), ((), ())), preferred_element_type=F32)
        tail_row = lax.broadcasted_iota(jnp.int32, (V_EXT - V_DIM, tm), 0)
        tail = jnp.where(tail_row == 0, 1.0, 0.0).astype(BF16)
        for hd in range(HEADS):
            v_ref[0, hd] = jnp.concatenate([vt[hd * V_DIM:(hd + 1) * V_DIM].astype(BF16), tail], axis=0)

    def recurrence_inputs():
        xr_ref[0] = z[:, C_XR:C_XR + LRU_WIDTH]
        gr_ref[0] = z[:, C_GR:C_GR + LRU_WIDTH]

    def spatial_gating():
        u = jax.nn.gelu(z[:, C_SU:C_SU + SGU_WIDTH])
        vg = jax.nn.gelu(z[:, C_SV:C_SV + SGU_WIDTH])
        v2 = vg * vg
        v2_hi = v2.astype(BF16)
        v2_lo = (v2 - v2_hi.astype(F32)).astype(BF16)
        gmat = gmat_ref[...]
        gms = _dot(v2_hi, gmat) + _dot(v2_lo, gmat)
        vb = (vg * lax.rsqrt(gms + EPS) * sgn_ref[...]).astype(BF16)
        grp = lax.broadcasted_iota(jnp.int32, (CHUNK, SGU_WIDTH), 1) // SGU_GROUP_DIM
        zero = jnp.zeros((CHUNK, SGU_WIDTH), BF16)
        wcat = wcat_ref[...]
        sbias = sbias_ref[...]
        parts = []
        for c in range(tm // CHUNK):
            vc = vb[c * CHUNK:(c + 1) * CHUNK]
            rhs = jnp.concatenate([jnp.where(grp == g, vc, zero) for g in range(SGU_GROUPS)], axis=0)
            s = _dot(wcat, rhs) + sbias
            parts.append(u[c * CHUNK:(c + 1) * CHUNK] * s)
        sp = jnp.concatenate(parts, axis=0) if len(parts) > 1 else parts[0]
        sp_ref[0] = _rms(sp, gsp_ref[...]).astype(BF16)

    queries()
    keys_values()
    recurrence_inputs()
    spatial_gating()


def _inproj_call(x, mod, mod_row, tabs, lw, l, tm):
    B, T, _ = x.shape
    cq, sa, sb, tk = tabs
    tab_spec = pl.BlockSpec((tm, LANE), lambda b, t: (t, 0))
    W = functools.partial(_layer_spec, l=l)
    in_specs = [
        pl.BlockSpec((1, tm, D_MODEL), lambda b, t: (b, t, 0)),
        _mod_spec(l, mod_row),
        W((1, D_MODEL)),
        W((D_MODEL, IN_EXT)),
        W((1, Q_LORA)),
        W((Q_LORA, HEADS * HEAD_PAD)),
        W((1, KV_LORA)),
        W((2 * LANE, HEADS * HEAD_PAD)),
        W((MLA_WIDTH, KV_LORA)),
        tab_spec, tab_spec, tab_spec, tab_spec,
        W((CHUNK, SGU_GROUPS * CHUNK)),
        W((CHUNK, SGU_WIDTH)),
        W((1, SGU_WIDTH)),
        _const_spec((SGU_WIDTH, SGU_WIDTH)),
        W((1, SGU_WIDTH)),
    ]
    out_shape = [
        jax.ShapeDtypeStruct((B, HEADS, T, HEAD_PAD), BF16),
        jax.ShapeDtypeStruct((B, HEADS, T, HEAD_PAD), BF16),
        jax.ShapeDtypeStruct((B, HEADS, V_EXT, T), BF16),
        jax.ShapeDtypeStruct((B, T, LRU_WIDTH), F32),
        jax.ShapeDtypeStruct((B, T, LRU_WIDTH), F32),
        jax.ShapeDtypeStruct((B, T, SGU_WIDTH), BF16),
    ]
    out_specs = [
        pl.BlockSpec((1, HEADS, tm, HEAD_PAD), lambda b, t: (b, 0, t, 0)),
        pl.BlockSpec((1, HEADS, tm, HEAD_PAD), lambda b, t: (b, 0, t, 0)),
        pl.BlockSpec((1, HEADS, V_EXT, tm), lambda b, t: (b, 0, 0, t)),
        pl.BlockSpec((1, tm, LRU_WIDTH), lambda b, t: (b, t, 0)),
        pl.BlockSpec((1, tm, LRU_WIDTH), lambda b, t: (b, t, 0)),
        pl.BlockSpec((1, tm, SGU_WIDTH), lambda b, t: (b, t, 0)),
    ]
    return pl.pallas_call(
        functools.partial(_inproj_kernel, tm=tm),
        out_shape=out_shape,
        grid=(B, T // tm),
        in_specs=in_specs,
        out_specs=out_specs,
        compiler_params=pltpu.CompilerParams(
            dimension_semantics=("parallel", "parallel"), vmem_limit_bytes=VMEM_LIMIT),
        name="inproj",
    )(x, mod, lw["norm1"], lw["w_in"], lw["q_a_norm"], lw["w_q"], lw["kv_a_norm"], lw["w_kv"], lw["w_vt"],
      cq, sa, sb, tk, lw["sgu_wcat"], lw["sgu_bias"], lw["sgu_norm"], lw["gmat"], lw["g_sp"])


GATE_ROWS = 256
SCAN_UNROLL = 8


def _tile_scan(a, b, h_prev, row, reverse):
    for k in (1, 2, 4):
        if reverse:
            keep = row < SUBLANE - k
            sh = SUBLANE - k
        else:
            keep = row >= k
            sh = k
        a_s = jnp.where(keep, pltpu.roll(a, sh, axis=0), 1.0)
        b_s = jnp.where(keep, pltpu.roll(b, sh, axis=0), 0.0)
        b = a * b_s + b
        a = a * a_s
    h = a * h_prev + b
    last = h[0:1] if reverse else h[SUBLANE - 1:SUBLANE]
    return h, jnp.broadcast_to(last, h.shape)


def _lru_kernel(xrc_ref, xrl_ref, cw_ref, cb_ref, wg_ref, bg_ref, lam_ref,
                outc_ref, outl_ref, xc_ref, a_ref, b_ref, hs_ref, *, tc, tl):
    cw = cw_ref[...]
    cb = cb_ref[...]
    ttot = tc + tl

    def conv(src_ref, n, dst0):
        x = src_ref[0]
        row = lax.broadcasted_iota(jnp.int32, (n, LRU_WIDTH), 0)
        y = cb
        for j in range(CONV_W):
            off = j - CONV_W // 2
            if off == 0:
                tap = x
            else:
                inside = row >= -off if off < 0 else row < n - off
                tap = jnp.where(inside, pltpu.roll(x, (-off) % n, axis=0), 0.0)
            y = y + cw[j:j + 1] * tap
        xc_ref[pl.ds(dst0, n), :] = y

    conv(xrc_ref, tc, 0)
    conv(xrl_ref, tl, tc)

    half_nsp = -0.5 * LRU_C * jax.nn.softplus(-lam_ref[...])
    wg = wg_ref[...]
    bg = bg_ref[...]
    tiles_per_chunk = GATE_ROWS // SUBLANE
    for c in range(ttot // GATE_ROWS):
        xcc = xc_ref[pl.ds(c * GATE_ROWS, GATE_ROWS), :]
        g = _dot(xcc.astype(BF16), wg) + bg
        for d in range(2):
            tr = jnp.tanh(g[:, (2 * d) * LRU_WIDTH:(2 * d + 1) * LRU_WIDTH])
            i = 0.5 * jnp.tanh(g[:, (2 * d + 1) * LRU_WIDTH:(2 * d + 2) * LRU_WIDTH]) + 0.5
            log_a = half_nsp[d:d + 1] * tr + half_nsp[d:d + 1]
            a = jnp.exp(log_a)
            bb = jnp.sqrt(-jnp.tanh(log_a) * (a * a + 1.0)) * (i * xcc)
            a_ref[d, pl.ds(c * tiles_per_chunk, tiles_per_chunk)] = a.reshape(
                tiles_per_chunk, SUBLANE, LRU_WIDTH)
            b_ref[d, pl.ds(c * tiles_per_chunk, tiles_per_chunk)] = bb.reshape(
                tiles_per_chunk, SUBLANE, LRU_WIDTH)

    row = lax.broadcasted_iota(jnp.int32, (SUBLANE, LRU_WIDTH), 0)
    nc = tc // SUBLANE
    nl = tl // SUBLANE

    def run(first_f, first_r, n, hf, hr):
        def body(i, carry):
            hf, hr = carry
            for j in range(SCAN_UNROLL):
                tf = first_f + i * SCAN_UNROLL + j
                tr = first_r - i * SCAN_UNROLL - j
                of, hf = _tile_scan(a_ref[0, tf], b_ref[0, tf], hf, row, False)
                orv, hr = _tile_scan(a_ref[1, tr], b_ref[1, tr], hr, row, True)
                hs_ref[0, tf] = of
                hs_ref[1, tr] = orv
            return hf, hr
        return lax.fori_loop(0, n // SCAN_UNROLL, body, (hf, hr))

    zero = jnp.zeros((SUBLANE, LRU_WIDTH), F32)
    hf, hr = run(0, nc - 1, nc, zero, zero)
    run(nc, nc + nl - 1, nl, hf, hr)

    for c in range(ttot // GATE_ROWS):
        sl = pl.ds(c * tiles_per_chunk, tiles_per_chunk)
        hsum = (hs_ref[0, sl] + hs_ref[1, sl]).reshape(GATE_ROWS, LRU_WIDTH)
        r0 = c * GATE_ROWS
        if r0 < tc:
            outc_ref[0, pl.ds(r0, GATE_ROWS), :] = hsum
        else:
            outl_ref[0, pl.ds(r0 - tc, GATE_ROWS), :] = hsum


def _lru_call(xr_c, xr_l, lw, l):
    B, tc, _ = xr_c.shape
    tl = xr_l.shape[1]
    ttot = tc + tl
    assert tc % GATE_ROWS == 0 and tl % GATE_ROWS == 0
    seq = lambda n: pl.BlockSpec((1, n, LRU_WIDTH), lambda b: (b, 0, 0))
    W = functools.partial(_layer_spec, l=l)
    return pl.pallas_call(
        functools.partial(_lru_kernel, tc=tc, tl=tl),
        out_shape=[jax.ShapeDtypeStruct((B, tc, LRU_WIDTH), F32),
                   jax.ShapeDtypeStruct((B, tl, LRU_WIDTH), F32)],
        grid=(B,),
        in_specs=[seq(tc), seq(tl),
                  W((CONV_W, LRU_WIDTH)), W((1, LRU_WIDTH)),
                  W((LRU_WIDTH, 4 * LRU_WIDTH)), W((1, 4 * LRU_WIDTH)),
                  W((2, LRU_WIDTH))],
        out_specs=[seq(tc), seq(tl)],
        scratch_shapes=[
            pltpu.VMEM((ttot, LRU_WIDTH), F32),
            pltpu.VMEM((2, ttot // SUBLANE, SUBLANE, LRU_WIDTH), F32),
            pltpu.VMEM((2, ttot // SUBLANE, SUBLANE, LRU_WIDTH), F32),
            pltpu.VMEM((2, ttot // SUBLANE, SUBLANE, LRU_WIDTH), F32),
        ],
        compiler_params=pltpu.CompilerParams(
            dimension_semantics=("parallel",), vmem_limit_bytes=VMEM_LIMIT),
        name="lru",
    )(xr_c, xr_l, lw["conv_w"], lw["conv_b"], lw["w_gates"], lw["b_gates"], lw["lru_lam"])


RED_ROWS = 64
ATTN_UNROLL = 8


def _col_reduce(x, op):
    rows, cols = x.shape
    part = op(x.reshape(rows // RED_ROWS, RED_ROWS, cols), axis=0)
    return op(part, axis=0, keepdims=True)


def _attn_kernel(*refs, nseg, tq):
    q_ref = refs[0]
    kv_refs = refs[1:1 + 2 * nseg]
    g_ref = refs[1 + 2 * nseg]
    o_ref = refs[2 + 2 * nseg]
    s_scr = refs[3 + 2 * nseg:5 + 2 * nseg]
    o_scr = refs[5 + 2 * nseg]
    seg_len = [kv_refs[2 * s].shape[2] for s in range(nseg)]
    seg_off = [sum(seg_len[:s]) for s in range(nseg)]

    nsub = q_ref.shape[2] // tq

    def scores_into(sub, hd, slot):
        q = q_ref[0, hd, pl.ds(pl.multiple_of(sub * tq, tq), tq), :]
        m = None
        for s in reversed(range(nseg)):
            sc = lax.dot_general(kv_refs[2 * s][0, hd], q, (((1,), (1,)), ((), ())),
                                 preferred_element_type=F32)
            s_scr[slot][pl.ds(seg_off[s], seg_len[s]), :] = sc
            ms = _col_reduce(sc, jnp.max)
            m = ms if m is None else jnp.maximum(m, ms)
        return m

    def softmax_pv(sub, hd, slot, m):
        o = None
        for s in reversed(range(nseg)):
            rows = pl.ds(seg_off[s], seg_len[s])
            p = jnp.exp2(s_scr[slot][rows, :] - m).astype(BF16)
            po = _dot(kv_refs[2 * s + 1][0, hd], p)
            o = po if o is None else o + po
        o_scr[sub, hd] = o[:V_DIM] * (1.0 / o[V_DIM:V_DIM + 1])

    def run_units(u0, n_units, m, has_next):
        for j in range(n_units):
            u = u0 + j
            if j + 1 < n_units or has_next:
                m_next = scores_into((u + 1) // HEADS, (u + 1) % HEADS, (j + 1) % 2)
            else:
                m_next = None
            softmax_pv(u // HEADS, u % HEADS, j % 2, m)
            m = m_next
        return m

    n_units = nsub * HEADS
    m_a = lax.fori_loop(0, n_units // ATTN_UNROLL - 1,
                        lambda i, m: run_units(i * ATTN_UNROLL, ATTN_UNROLL, m, True),
                        scores_into(0, 0, 0))
    run_units(n_units - ATTN_UNROLL, ATTN_UNROLL, m_a, False)
    for sub in range(nsub):
        att = o_scr[sub].reshape(MLA_WIDTH, tq).T
        o_ref[0, pl.ds(sub * tq, tq), :] = _rms(att, g_ref[...]).astype(BF16)


def _attn_call(q, segs, g_att, l, tq_blk, tq):
    B, _, S, _ = q.shape
    ttot = sum(k.shape[2] for k, _, _ in segs)
    in_specs = [pl.BlockSpec((1, HEADS, tq_blk, HEAD_PAD), lambda b, t: (b, 0, t, 0))]
    args = [q]
    for k, v, parts in segs:
        tn = k.shape[2] // parts
        for j in range(parts):
            in_specs.append(pl.BlockSpec((1, HEADS, tn, HEAD_PAD), lambda b, t, j=j: (b, 0, j, 0)))
            in_specs.append(pl.BlockSpec((1, HEADS, V_EXT, tn), lambda b, t, j=j: (b, 0, 0, j)))
            args += [k, v]
    nseg = (len(args) - 1) // 2
    in_specs.append(_layer_spec((1, MLA_WIDTH), l))
    args.append(g_att)
    return pl.pallas_call(
        functools.partial(_attn_kernel, nseg=nseg, tq=tq),
        out_shape=jax.ShapeDtypeStruct((B, S, MLA_WIDTH), BF16),
        grid=(B, S // tq_blk),
        in_specs=in_specs,
        out_specs=pl.BlockSpec((1, tq_blk, MLA_WIDTH), lambda b, t: (b, t, 0)),
        scratch_shapes=[pltpu.VMEM((ttot, tq), F32), pltpu.VMEM((ttot, tq), F32),
                        pltpu.VMEM((tq_blk // tq, HEADS, V_DIM, tq), F32)],
        compiler_params=pltpu.CompilerParams(
            dimension_semantics=("parallel", "arbitrary"), vmem_limit_bytes=VMEM_LIMIT),
        name="attn",
    )(*args)


def _out_kernel(x_ref, att_ref, hs_ref, gr_ref, sp_ref, mod_ref, n2_ref, grec_ref, wo_ref, wfi_ref, wfo_ref,
                fn_ref, o_ref, *, final):
    mod = mod_ref[...]
    rows = x_ref.shape[1] // OUT_SPLIT
    x1s, h2s = [], []
    for r in range(OUT_SPLIT):
        sl = pl.ds(r * rows, rows)
        rec = _rms(hs_ref[0, sl, :] * jax.nn.gelu(gr_ref[0, sl, :]), grec_ref[...]).astype(BF16)
        y = _dot(att_ref[0, sl, :], wo_ref[pl.ds(0, MLA_WIDTH), :])
        y = y + _dot(rec, wo_ref[pl.ds(MLA_WIDTH, LRU_WIDTH), :])
        y = y + _dot(sp_ref[0, sl, :], wo_ref[pl.ds(MLA_WIDTH + LRU_WIDTH, SGU_WIDTH), :])
        x1 = x_ref[0, sl, :] + mod[2:3] * y
        x1s.append(x1)
        h2s.append((_rms(x1, n2_ref[...]) * (1.0 + mod[4:5]) + mod[3:4]).astype(BF16))
    for r in range(OUT_SPLIT):
        acc = None
        for c0, cw in FF_CHUNKS:
            g = _dot(h2s[r], wfi_ref[:, pl.ds(c0, cw)])
            u = _dot(h2s[r], wfi_ref[:, pl.ds(D_FF + c0, cw)])
            act = (jax.nn.silu(g) * u).astype(BF16)
            part = _dot(act, wfo_ref[pl.ds(c0, cw), :])
            acc = part if acc is None else acc + part
        x2 = x1s[r] + mod[5:6] * acc
        if final:
            x2 = _rms(x2, fn_ref[...])
        o_ref[0, pl.ds(r * rows, rows), :] = x2


def _out_call(x, att, hsum, gr, sp, mod, mod_row, lw, l, final_norm, final, tm):
    B, T, _ = x.shape
    tok = lambda w: pl.BlockSpec((1, tm, w), lambda b, t: (b, t, 0))
    W = functools.partial(_layer_spec, l=l)
    return pl.pallas_call(
        functools.partial(_out_kernel, final=final),
        out_shape=jax.ShapeDtypeStruct((B, T, D_MODEL), F32),
        grid=(B, T // tm),
        in_specs=[tok(D_MODEL), tok(MLA_WIDTH), tok(LRU_WIDTH), tok(LRU_WIDTH), tok(SGU_WIDTH),
                  _mod_spec(l, mod_row),
                  W((1, D_MODEL)),
                  W((1, LRU_WIDTH)),
                  W((D_MODEL, D_MODEL)),
                  W((D_MODEL, 2 * D_FF)),
                  W((D_FF, D_MODEL)),
                  _const_spec((1, D_MODEL))],
        out_specs=tok(D_MODEL),
        compiler_params=pltpu.CompilerParams(
            dimension_semantics=("parallel", "parallel"), vmem_limit_bytes=VMEM_LIMIT),
        name="out_ffn",
    )(x, att, hsum, gr, sp, mod, lw["norm2"], lw["g_rec"], lw["w_out"], lw["w_ffn_in"], lw["w_ffn_out"],
      final_norm)


def _rot_cols(w):
    q = ROPE // 4
    return jnp.concatenate([-w[..., q:2 * q], w[..., 0:q], -w[..., 3 * q:4 * q], w[..., 2 * q:3 * q]], axis=-1)


def _prep_params(p):
    L = DEPTH
    splits = np.cumsum([Q_LORA, KV_LORA, ROPE, LRU_WIDTH, LRU_WIDTH, SGU_WIDTH])
    qa, kva, kr, xr, gr, su, sv = jnp.split(p["w_in"].astype(BF16), splits, axis=2)
    kr_blk = jnp.concatenate([kr, _rot_cols(kr), jnp.zeros((L, D_MODEL, LANE - 2 * ROPE), BF16)], axis=2)
    w_in_ext = jnp.concatenate([qa, kva, kr_blk, xr, gr, su, sv], axis=2)

    wq = p["w_q_b"].astype(BF16).reshape(L, Q_LORA, HEADS, NOPE + ROPE)
    zpad = jnp.zeros((L, Q_LORA, HEADS, HEAD_PAD - NOPE - ROPE), BF16)
    w_q = jnp.concatenate([wq, zpad], axis=-1).reshape(L, Q_LORA, HEADS * HEAD_PAD)

    wkv = p["w_kv_b"].astype(BF16).reshape(L, KV_LORA, HEADS, NOPE + V_DIM)
    wk = jnp.concatenate([wkv[..., :NOPE], jnp.zeros((L, KV_LORA, HEADS, HEAD_PAD - NOPE), BF16)],
                         axis=-1).reshape(L, KV_LORA, HEADS * HEAD_PAD)
    wv = wkv[..., NOPE:].reshape(L, KV_LORA, MLA_WIDTH)
    place = np.zeros((LANE, HEADS, HEAD_PAD), np.float32)
    place[np.arange(ROPE), :, NOPE + np.arange(ROPE)] = 1.0
    place = jnp.broadcast_to(jnp.asarray(place.reshape(LANE, HEADS * HEAD_PAD), dtype=BF16),
                             (L, LANE, HEADS * HEAD_PAD))
    w_kv = jnp.concatenate([wk, place], axis=1)
    w_vt = jnp.swapaxes(wv, 1, 2)

    eye = np.eye(LRU_HEADS, dtype=np.float32)[None, :, None, :, None]

    def block_diag(w):
        _, hh, bi, bj = w.shape
        return (eye * w[:, :, :, None, :]).reshape(L, hh * bi, hh * bj)

    w_gates = (0.5 * jnp.concatenate([block_diag(p["lru_w_r"][:, 0]), block_diag(p["lru_w_i"][:, 0]),
                                      block_diag(p["lru_w_r"][:, 1]), block_diag(p["lru_w_i"][:, 1])],
                                     axis=2)).astype(BF16)
    b_gates = 0.5 * jnp.concatenate([p["lru_b_r"][:, 0], p["lru_b_i"][:, 0],
                                     p["lru_b_r"][:, 1], p["lru_b_i"][:, 1]], axis=1)[:, None]

    sgu_wcat = jnp.transpose(p["sgu_w"], (0, 2, 1, 3)).reshape(L, CHUNK, SGU_GROUPS * CHUNK).astype(BF16)
    sgu_bias = jnp.repeat(jnp.swapaxes(p["sgu_b"], 1, 2), SGU_GROUP_DIM, axis=2)
    gidx = np.arange(SGU_WIDTH) // SGU_GROUP_DIM
    gmat = jnp.asarray((gidx[:, None] == gidx[None, :]).astype(np.float32) / SGU_GROUP_DIM, dtype=BF16)

    on = p["out_norm"][:, None]
    row = lambda a: a[:, None]
    return {
        "norm1": row(p["norm1"]), "norm2": row(p["norm2"]),
        "w_in": w_in_ext, "q_a_norm": row(p["q_a_norm"]), "w_q": w_q,
        "kv_a_norm": row(p["kv_a_norm"]), "w_kv": w_kv, "w_vt": w_vt,
        "conv_w": p["conv_w"], "conv_b": row(p["conv_b"]),
        "w_gates": w_gates, "b_gates": b_gates, "lru_lam": p["lru_lam"],
        "sgu_wcat": sgu_wcat, "sgu_bias": sgu_bias, "sgu_norm": row(p["sgu_norm"]), "gmat": gmat,
        "g_att": on[..., :MLA_WIDTH], "g_rec": on[..., MLA_WIDTH:MLA_WIDTH + LRU_WIDTH],
        "g_sp": on[..., MLA_WIDTH + LRU_WIDTH:],
        "w_out": p["w_out"].astype(BF16), "w_ffn_in": p["w_ffn_in"].astype(BF16),
        "w_ffn_out": p["w_ffn_out"].astype(BF16),
    }


def _rope_tables(seq_len, ctx_len):
    f32 = np.float32
    rows = seq_len // GRID_W
    row = np.repeat(np.arange(rows, dtype=f32), GRID_W)
    col = np.tile(np.arange(GRID_W, dtype=f32), rows)
    half = ROPE // 2
    freq = (f32(ROPE_BASE) ** (-np.arange(0, half, 2, dtype=f32) / f32(half))).astype(f32)
    ar = row[:, None] * freq
    ac = col[:, None] * freq
    ang = np.concatenate([ar, ar, ac, ac], axis=-1).astype(f32)
    cos, sin = np.cos(ang).astype(f32), np.sin(ang).astype(f32)
    qs = f32(math.log2(math.e) / math.sqrt(NOPE + ROPE))

    up = ((np.arange(ROPE) // ROT_SHIFT) % 2 == 0).astype(f32)

    def q_tabs(c, s, n):
        pad = np.zeros((n, HEAD_PAD - NOPE - ROPE), f32)
        nope0 = np.zeros((n, NOPE), f32)
        cq = np.concatenate([np.ones((n, NOPE), f32), c, pad], axis=1)
        sa = np.concatenate([nope0, -s * up, pad], axis=1)
        sb = np.concatenate([nope0, s * (1 - up), pad], axis=1)
        tk = np.concatenate([c, s, np.zeros((n, LANE - 2 * ROPE), f32)], axis=1)
        return jnp.asarray(cq * qs), jnp.asarray(sa * qs), jnp.asarray(sb * qs), jnp.asarray(tk)

    lat = q_tabs(cos, sin, seq_len)
    ctx = q_tabs(np.ones((ctx_len, ROPE), f32), np.zeros((ctx_len, ROPE), f32), ctx_len)
    return lat, ctx


def kernel(x, c, ctx, c_ctx, norm1, norm2, w_ada, b_ada, w_in, q_a_norm, w_q_b, kv_a_norm, w_kv_b, conv_w, conv_b, lru_w_r, lru_b_r, lru_w_i, lru_b_i, lru_lam, sgu_norm, sgu_w, sgu_b, out_norm, w_out, w_ffn_in, w_ffn_out, final_norm):
    p = dict(norm1=norm1, norm2=norm2, w_in=w_in, q_a_norm=q_a_norm, w_q_b=w_q_b, kv_a_norm=kv_a_norm,
             w_kv_b=w_kv_b, conv_w=conv_w, conv_b=conv_b, lru_w_r=lru_w_r, lru_b_r=lru_b_r,
             lru_w_i=lru_w_i, lru_b_i=lru_b_i, lru_lam=lru_lam, sgu_norm=sgu_norm, sgu_w=sgu_w,
             sgu_b=sgu_b, out_norm=out_norm, w_out=w_out, w_ffn_in=w_ffn_in, w_ffn_out=w_ffn_out)
    B, S, _ = x.shape
    tctx = ctx.shape[1]
    mod_rows = ((B + 1 + SUBLANE - 1) // SUBLANE) * SUBLANE
    cc = jnp.concatenate([c, c_ctx[None], jnp.zeros((mod_rows - B - 1, D_MODEL), F32)], axis=0)
    mod = _ada_call(cc, w_ada, b_ada).reshape(DEPTH, mod_rows, 6, D_MODEL)
    tabs_l, tabs_c = _rope_tables(S, tctx)
    fn = final_norm[None]
    lw = _prep_params(p)

    h_ctx = ctx
    for l in range(DEPTH):
        last = l == DEPTH - 1
        q_l, k_l, v_l, xr_l, gr_l, sp_l = _inproj_call(x, mod, None, tabs_l, lw, l, 1024)
        q_c, k_c, v_c, xr_c, gr_c, sp_c = _inproj_call(h_ctx, mod, B, tabs_c, lw, l, 256)
        hs_c, hs_l = _lru_call(xr_c, xr_l, lw, l)
        att_l = _attn_call(q_l, [(k_c, v_c, 1), (k_l, v_l, 2)], lw["g_att"], l, 2048, 256)
        x = _out_call(x, att_l, hs_l, gr_l, sp_l, mod, None, lw, l, fn, last, 512)
        if not last:
            att_c = _attn_call(q_c, [(k_c, v_c, 1)], lw["g_att"], l, 256, 256)
            h_ctx = _out_call(h_ctx, att_c, hs_c, gr_c, sp_c, mod, B, lw, l, fn, False, 256)
    return x
```

```python
import functools
import math

import jax
import jax.numpy as jnp
import numpy as np
from jax import lax
from jax.experimental import pallas as pl
from jax.experimental.pallas import tpu as pltpu

F32 = jnp.float32
BF16 = jnp.bfloat16

D_MODEL = 1024
DEPTH = 2
GRID_W = 64
EPS = 1e-6
ROPE_BASE = 10000.0
HEADS = 8
NOPE = 64
ROPE = 32
V_DIM = 64
Q_LORA = 256
KV_LORA = 128
LRU_WIDTH = 256
LRU_HEADS = 4
CONV_W = 4
LRU_C = 8.0
SGU_GROUPS = 4
SGU_WIDTH = 256
SGU_GROUP_DIM = SGU_WIDTH // SGU_GROUPS
CHUNK = 128
D_FF = 2816
MLA_WIDTH = HEADS * V_DIM
V_EXT = V_DIM + 16

LANE = 128
SUBLANE = 8
HEAD_PAD = LANE
ROT_SHIFT = ROPE // 4
VMEM_LIMIT = 56 * 1024 * 1024

IN_EXT = 1536
C_QA, C_KVA, C_KR, C_XR, C_GR, C_SU, C_SV = 0, 256, 384, 512, 768, 1024, 1280
MXU_TILE = 256
FF_CHUNKS = ((0, 6 * MXU_TILE), (6 * MXU_TILE, 5 * MXU_TILE))
OUT_ROWS = 256
NORM_ROWS = 128


def _rms(x, g):
    ms = jnp.mean(x * x, axis=-1, keepdims=True)
    return x * lax.rsqrt(ms + EPS) * g


def _dot(a, b):
    return jnp.dot(a, b, preferred_element_type=F32)


def _const_spec(shape):
    zeros = (0,) * len(shape)
    return pl.BlockSpec(shape, lambda *_: zeros, pipeline_mode=pl.Buffered(1))


def _layer_spec(shape, l):
    zeros = (0,) * len(shape)
    return pl.BlockSpec((None,) + tuple(shape), lambda *_: (l,) + zeros, pipeline_mode=pl.Buffered(1))


def _mod_spec(l, row):
    if row is None:
        return pl.BlockSpec((None, None, 6, D_MODEL), lambda b, t: (l, b, 0, 0))
    return pl.BlockSpec((None, None, 6, D_MODEL), lambda b, t: (l, row, 0, 0))


ADA_TILE = 512


def _ada_kernel(c_ref, w_ref, b_ref, o_ref):
    s = jax.nn.silu(c_ref[...]).astype(BF16)
    o_ref[0] = _dot(s, w_ref[0].astype(BF16)) + b_ref[0]


def _ada_call(cc, w_ada, b_ada):
    rows = cc.shape[0]
    n = w_ada.shape[-1]
    return pl.pallas_call(
        _ada_kernel,
        out_shape=jax.ShapeDtypeStruct((DEPTH, rows, n), F32),
        grid=(DEPTH, n // ADA_TILE),
        in_specs=[
            pl.BlockSpec((rows, D_MODEL), lambda l, j: (0, 0)),
            pl.BlockSpec((1, D_MODEL, ADA_TILE), lambda l, j: (l, 0, j)),
            pl.BlockSpec((1, 1, ADA_TILE), lambda l, j: (l, 0, j)),
        ],
        out_specs=pl.BlockSpec((1, rows, ADA_TILE), lambda l, j: (l, 0, j)),
        compiler_params=pltpu.CompilerParams(
            dimension_semantics=("arbitrary", "arbitrary"), vmem_limit_bytes=VMEM_LIMIT),
        name="ada_mod",
    )(cc, w_ada, b_ada.reshape(DEPTH, 1, n))


def _inproj_kernel(x_ref, mod_ref, n1_ref, win_ref, qan_ref, wq_ref, kvn_ref, wkv_ref, wvt_ref,
                   cq_ref, sa_ref, sb_ref, tk_ref, wcat_ref, sbias_ref, sgn_ref, gmat_ref, gsp_ref,
                   q_ref, k_ref, v_ref, xr_ref, gr_ref, sp_ref, *, tm):
    mod = mod_ref[...]
    zs = []
    for k in range(tm // NORM_ROWS):
        xs = x_ref[0, pl.ds(k * NORM_ROWS, NORM_ROWS), :]
        h = _rms(xs, n1_ref[...]) * (1.0 + mod[1:2]) + mod[0:1]
        zs.append(_dot(h.astype(BF16), win_ref[...]))
    z = jnp.concatenate(zs, axis=0)

    def queries():
        qn = _rms(z[:, C_QA:C_QA + Q_LORA], qan_ref[...]).astype(BF16)
        qq = _dot(qn, wq_ref[...])
        cq = cq_ref[...]
        sa = sa_ref[...]
        sb = sb_ref[...]
        for hd in range(HEADS):
            qh = qq[:, hd * HEAD_PAD:(hd + 1) * HEAD_PAD]
            qh = (qh * cq + pltpu.roll(qh, LANE - ROT_SHIFT, axis=1) * sa
                  + pltpu.roll(qh, ROT_SHIFT, axis=1) * sb)
            q_ref[0, hd] = qh.astype(BF16)

    def keys_values():
        kvn = _rms(z[:, C_KVA:C_KVA + KV_LORA], kvn_ref[...]).astype(BF16)
        t = z[:, C_KR:C_KR + LANE] * tk_ref[...]
        kro = t + pltpu.roll(t, LANE - ROPE, axis=1)
        lhs = jnp.concatenate([kvn, kro.astype(BF16)], axis=1)
        kk = _dot(lhs, wkv_ref[...])
        for hd in range(HEADS):
            k_ref[0, hd] = kk[:, hd * HEAD_PAD:(hd + 1) * HEAD_PAD].astype(BF16)
        vt = lax.dot_general(wvt_ref[...], kvn, (((1,), (1,)), ((), ())), preferred_element_type=F32)
        tail_row = lax.broadcasted_iota(jnp.int32, (V_EXT - V_DIM, tm), 0)
        tail = jnp.where(tail_row == 0, 1.0, 0.0).astype(BF16)
        for hd in range(HEADS):
            v_ref[0, hd] = jnp.concatenate([vt[hd * V_DIM:(hd + 1) * V_DIM].astype(BF16), tail], axis=0)

    def recurrence_inputs():
        xr_ref[0] = z[:, C_XR:C_XR + LRU_WIDTH]
        gr_ref[0] = z[:, C_GR:C_GR + LRU_WIDTH]

    def spatial_gating():
        u = jax.nn.gelu(z[:, C_SU:C_SU + SGU_WIDTH])
        vg = jax.nn.gelu(z[:, C_SV:C_SV + SGU_WIDTH])
        v2 = vg * vg
        v2_hi = v2.astype(BF16)
        v2_lo = (v2 - v2_hi.astype(F32)).astype(BF16)
        gmat = gmat_ref[...]
        gms = _dot(v2_hi, gmat) + _dot(v2_lo, gmat)
        vb = (vg * lax.rsqrt(gms + EPS) * sgn_ref[...]).astype(BF16)
        grp = lax.broadcasted_iota(jnp.int32, (CHUNK, SGU_WIDTH), 1) // SGU_GROUP_DIM
        zero = jnp.zeros((CHUNK, SGU_WIDTH), BF16)
        wcat = wcat_ref[...]
        sbias = sbias_ref[...]
        parts = []
        for c in range(tm // CHUNK):
            vc = vb[c * CHUNK:(c + 1) * CHUNK]
            rhs = jnp.concatenate([jnp.where(grp == g, vc, zero) for g in range(SGU_GROUPS)], axis=0)
            s = _dot(wcat, rhs) + sbias
            parts.append(u[c * CHUNK:(c + 1) * CHUNK] * s)
        sp = jnp.concatenate(parts, axis=0) if len(parts) > 1 else parts[0]
        sp_ref[0] = _rms(sp, gsp_ref[...]).astype(BF16)

    queries()
    keys_values()
    recurrence_inputs()
    spatial_gating()


def _inproj_call(x, mod, mod_row, tabs, lw, l, tm):
    B, T, _ = x.shape
    cq, sa, sb, tk = tabs
    tab_spec = pl.BlockSpec((tm, LANE), lambda b, t: (t, 0))
    W = functools.partial(_layer_spec, l=l)
    in_specs = [
        pl.BlockSpec((1, tm, D_MODEL), lambda b, t: (b, t, 0)),
        _mod_spec(l, mod_row),
        W((1, D_MODEL)),
        W((D_MODEL, IN_EXT)),
        W((1, Q_LORA)),
        W((Q_LORA, HEADS * HEAD_PAD)),
        W((1, KV_LORA)),
        W((2 * LANE, HEADS * HEAD_PAD)),
        W((MLA_WIDTH, KV_LORA)),
        tab_spec, tab_spec, tab_spec, tab_spec,
        W((CHUNK, SGU_GROUPS * CHUNK)),
        W((CHUNK, SGU_WIDTH)),
        W((1, SGU_WIDTH)),
        _const_spec((SGU_WIDTH, SGU_WIDTH)),
        W((1, SGU_WIDTH)),
    ]
    out_shape = [
        jax.ShapeDtypeStruct((B, HEADS, T, HEAD_PAD), BF16),
        jax.ShapeDtypeStruct((B, HEADS, T, HEAD_PAD), BF16),
        jax.ShapeDtypeStruct((B, HEADS, V_EXT, T), BF16),
        jax.ShapeDtypeStruct((B, T, LRU_WIDTH), F32),
        jax.ShapeDtypeStruct((B, T, LRU_WIDTH), F32),
        jax.ShapeDtypeStruct((B, T, SGU_WIDTH), BF16),
    ]
    out_specs = [
        pl.BlockSpec((1, HEADS, tm, HEAD_PAD), lambda b, t: (b, 0, t, 0)),
        pl.BlockSpec((1, HEADS, tm, HEAD_PAD), lambda b, t: (b, 0, t, 0)),
        pl.BlockSpec((1, HEADS, V_EXT, tm), lambda b, t: (b, 0, 0, t)),
        pl.BlockSpec((1, tm, LRU_WIDTH), lambda b, t: (b, t, 0)),
        pl.BlockSpec((1, tm, LRU_WIDTH), lambda b, t: (b, t, 0)),
        pl.BlockSpec((1, tm, SGU_WIDTH), lambda b, t: (b, t, 0)),
    ]
    return pl.pallas_call(
        functools.partial(_inproj_kernel, tm=tm),
        out_shape=out_shape,
        grid=(B, T // tm),
        in_specs=in_specs,
        out_specs=out_specs,
        compiler_params=pltpu.CompilerParams(
            dimension_semantics=("parallel", "parallel"), vmem_limit_bytes=VMEM_LIMIT),
        name="inproj",
    )(x, mod, lw["norm1"], lw["w_in"], lw["q_a_norm"], lw["w_q"], lw["kv_a_norm"], lw["w_kv"], lw["w_vt"],
      cq, sa, sb, tk, lw["sgu_wcat"], lw["sgu_bias"], lw["sgu_norm"], lw["gmat"], lw["g_sp"])


GATE_ROWS = 256
SCAN_UNROLL = 8


def _tile_scan(a, b, h_prev, row, reverse):
    for k in (1, 2, 4):
        if reverse:
            keep = row < SUBLANE - k
            sh = SUBLANE - k
        else:
            keep = row >= k
            sh = k
        a_s = jnp.where(keep, pltpu.roll(a, sh, axis=0), 1.0)
        b_s = jnp.where(keep, pltpu.roll(b, sh, axis=0), 0.0)
        b = a * b_s + b
        a = a * a_s
    h = a * h_prev + b
    last = h[0:1] if reverse else h[SUBLANE - 1:SUBLANE]
    return h, jnp.broadcast_to(last, h.shape)


def _lru_kernel(xrc_ref, xrl_ref, cw_ref, cb_ref, wg_ref, bg_ref, lam_ref,
                outc_ref, outl_ref, xc_ref, a_ref, b_ref, hs_ref, *, tc, tl):
    cw = cw_ref[...]
    cb = cb_ref[...]
    ttot = tc + tl

    def conv(src_ref, n, dst0):
        x = src_ref[0]
        row = lax.broadcasted_iota(jnp.int32, (n, LRU_WIDTH), 0)
        y = cb
        for j in range(CONV_W):
            off = j - CONV_W // 2
            if off == 0:
                tap = x
            else:
                inside = row >= -off if off < 0 else row < n - off
                tap = jnp.where(inside, pltpu.roll(x, (-off) % n, axis=0), 0.0)
            y = y + cw[j:j + 1] * tap
        xc_ref[pl.ds(dst0, n), :] = y

    conv(xrc_ref, tc, 0)
    conv(xrl_ref, tl, tc)

    half_nsp = -0.5 * LRU_C * jax.nn.softplus(-lam_ref[...])
    wg = wg_ref[...]
    bg = bg_ref[...]
    tiles_per_chunk = GATE_ROWS // SUBLANE
    for c in range(ttot // GATE_ROWS):
        xcc = xc_ref[pl.ds(c * GATE_ROWS, GATE_ROWS), :]
        g = _dot(xcc.astype(BF16), wg) + bg
        for d in range(2):
            tr = jnp.tanh(g[:, (2 * d) * LRU_WIDTH:(2 * d + 1) * LRU_WIDTH])
            i = 0.5 * jnp.tanh(g[:, (2 * d + 1) * LRU_WIDTH:(2 * d + 2) * LRU_WIDTH]) + 0.5
            log_a = half_nsp[d:d + 1] * tr + half_nsp[d:d + 1]
            a = jnp.exp(log_a)
            bb = jnp.sqrt(-jnp.tanh(log_a) * (a * a + 1.0)) * (i * xcc)
            a_ref[d, pl.ds(c * tiles_per_chunk, tiles_per_chunk)] = a.reshape(
                tiles_per_chunk, SUBLANE, LRU_WIDTH)
            b_ref[d, pl.ds(c * tiles_per_chunk, tiles_per_chunk)] = bb.reshape(
                tiles_per_chunk, SUBLANE, LRU_WIDTH)

    row = lax.broadcasted_iota(jnp.int32, (SUBLANE, LRU_WIDTH), 0)
    nc = tc // SUBLANE
    nl = tl // SUBLANE

    def run(first_f, first_r, n, hf, hr):
        def body(i, carry):
            hf, hr = carry
            for j in range(SCAN_UNROLL):
                tf = first_f + i * SCAN_UNROLL + j
                tr = first_r - i * SCAN_UNROLL - j
                of, hf = _tile_scan(a_ref[0, tf], b_ref[0, tf], hf, row, False)
                orv, hr = _tile_scan(a_ref[1, tr], b_ref[1, tr], hr, row, True)
                hs_ref[0, tf] = of
                hs_ref[1, tr] = orv
            return hf, hr
        return lax.fori_loop(0, n // SCAN_UNROLL, body, (hf, hr))

    zero = jnp.zeros((SUBLANE, LRU_WIDTH), F32)
    hf, hr = run(0, nc - 1, nc, zero, zero)
    run(nc, nc + nl - 1, nl, hf, hr)

    for c in range(ttot // GATE_ROWS):
        sl = pl.ds(c * tiles_per_chunk, tiles_per_chunk)
        hsum = (hs_ref[0, sl] + hs_ref[1, sl]).reshape(GATE_ROWS, LRU_WIDTH)
        r0 = c * GATE_ROWS
        if r0 < tc:
            outc_ref[0, pl.ds(r0, GATE_ROWS), :] = hsum
        else:
            outl_ref[0, pl.ds(r0 - tc, GATE_ROWS), :] = hsum


def _lru_call(xr_c, xr_l, lw, l):
    B, tc, _ = xr_c.shape
    tl = xr_l.shape[1]
    ttot = tc + tl
    assert tc % GATE_ROWS == 0 and tl % GATE_ROWS == 0
    seq = lambda n: pl.BlockSpec((1, n, LRU_WIDTH), lambda b: (b, 0, 0))
    W = functools.partial(_layer_spec, l=l)
    return pl.pallas_call(
        functools.partial(_lru_kernel, tc=tc, tl=tl),
        out_shape=[jax.ShapeDtypeStruct((B, tc, LRU_WIDTH), F32),
                   jax.ShapeDtypeStruct((B, tl, LRU_WIDTH), F32)],
        grid=(B,),
        in_specs=[seq(tc), seq(tl),
                  W((CONV_W, LRU_WIDTH)), W((1, LRU_WIDTH)),
                  W((LRU_WIDTH, 4 * LRU_WIDTH)), W((1, 4 * LRU_WIDTH)),
                  W((2, LRU_WIDTH))],
        out_specs=[seq(tc), seq(tl)],
        scratch_shapes=[
            pltpu.VMEM((ttot, LRU_WIDTH), F32),
            pltpu.VMEM((2, ttot // SUBLANE, SUBLANE, LRU_WIDTH), F32),
            pltpu.VMEM((2, ttot // SUBLANE, SUBLANE, LRU_WIDTH), F32),
            pltpu.VMEM((2, ttot // SUBLANE, SUBLANE, LRU_WIDTH), F32),
        ],
        compiler_params=pltpu.CompilerParams(
            dimension_semantics=("parallel",), vmem_limit_bytes=VMEM_LIMIT),
        name="lru",
    )(xr_c, xr_l, lw["conv_w"], lw["conv_b"], lw["w_gates"], lw["b_gates"], lw["lru_lam"])


RED_ROWS = 64
ATTN_UNROLL = 8


def _col_reduce(x, op):
    rows, cols = x.shape
    part = op(x.reshape(rows // RED_ROWS, RED_ROWS, cols), axis=0)
    return op(part, axis=0, keepdims=True)


def _attn_kernel(*refs, nseg, tq):
    q_ref = refs[0]
    kv_refs = refs[1:1 + 2 * nseg]
    g_ref = refs[1 + 2 * nseg]
    o_ref = refs[2 + 2 * nseg]
    s_scr = refs[3 + 2 * nseg:5 + 2 * nseg]
    o_scr = refs[5 + 2 * nseg]
    seg_len = [kv_refs[2 * s].shape[2] for s in range(nseg)]
    seg_off = [sum(seg_len[:s]) for s in range(nseg)]

    nsub = q_ref.shape[2] // tq

    def scores_into(sub, hd, slot):
        q = q_ref[0, hd, pl.ds(pl.multiple_of(sub * tq, tq), tq), :]
        m = None
        for s in reversed(range(nseg)):
            sc = lax.dot_general(kv_refs[2 * s][0, hd], q, (((1,), (1,)), ((), ())),
                                 preferred_element_type=F32)
            s_scr[slot][pl.ds(seg_off[s], seg_len[s]), :] = sc
            ms = _col_reduce(sc, jnp.max)
            m = ms if m is None else jnp.maximum(m, ms)
        return m

    def softmax_pv(sub, hd, slot, m):
        o = None
        for s in reversed(range(nseg)):
            rows = pl.ds(seg_off[s], seg_len[s])
            p = jnp.exp2(s_scr[slot][rows, :] - m).astype(BF16)
            po = _dot(kv_refs[2 * s + 1][0, hd], p)
            o = po if o is None else o + po
        o_scr[sub, hd] = o[:V_DIM] * (1.0 / o[V_DIM:V_DIM + 1])

    def run_units(u0, n_units, m, has_next):
        for j in range(n_units):
            u = u0 + j
            if j + 1 < n_units or has_next:
                m_next = scores_into((u + 1) // HEADS, (u + 1) % HEADS, (j + 1) % 2)
            else:
                m_next = None
            softmax_pv(u // HEADS, u % HEADS, j % 2, m)
            m = m_next
        return m

    n_units = nsub * HEADS
    m_a = lax.fori_loop(0, n_units // ATTN_UNROLL - 1,
                        lambda i, m: run_units(i * ATTN_UNROLL, ATTN_UNROLL, m, True),
                        scores_into(0, 0, 0))
    run_units(n_units - ATTN_UNROLL, ATTN_UNROLL, m_a, False)
    for sub in range(nsub):
        att = o_scr[sub].reshape(MLA_WIDTH, tq).T
        o_ref[0, pl.ds(sub * tq, tq), :] = _rms(att, g_ref[...]).astype(BF16)


def _attn_call(q, segs, g_att, l, tq_blk, tq):
    B, _, S, _ = q.shape
    ttot = sum(k.shape[2] for k, _, _ in segs)
    in_specs = [pl.BlockSpec((1, HEADS, tq_blk, HEAD_PAD), lambda b, t: (b, 0, t, 0))]
    args = [q]
    for k, v, parts in segs:
        tn = k.shape[2] // parts
        for j in range(parts):
            in_specs.append(pl.BlockSpec((1, HEADS, tn, HEAD_PAD), lambda b, t, j=j: (b, 0, j, 0)))
            in_specs.append(pl.BlockSpec((1, HEADS, V_EXT, tn), lambda b, t, j=j: (b, 0, 0, j)))
            args += [k, v]
    nseg = (len(args) - 1) // 2
    in_specs.append(_layer_spec((1, MLA_WIDTH), l))
    args.append(g_att)
    return pl.pallas_call(
        functools.partial(_attn_kernel, nseg=nseg, tq=tq),
        out_shape=jax.ShapeDtypeStruct((B, S, MLA_WIDTH), BF16),
        grid=(B, S // tq_blk),
        in_specs=in_specs,
        out_specs=pl.BlockSpec((1, tq_blk, MLA_WIDTH), lambda b, t: (b, t, 0)),
        scratch_shapes=[pltpu.VMEM((ttot, tq), F32), pltpu.VMEM((ttot, tq), F32),
                        pltpu.VMEM((tq_blk // tq, HEADS, V_DIM, tq), F32)],
        compiler_params=pltpu.CompilerParams(
            dimension_semantics=("parallel", "arbitrary"), vmem_limit_bytes=VMEM_LIMIT),
        name="attn",
    )(*args)


def _out_kernel(x_ref, att_ref, hs_ref, gr_ref, sp_ref, mod_ref, n2_ref, grec_ref, wo_ref, wfi_ref, wfo_ref,
                fn_ref, o_ref, *, final):
    mod = mod_ref[...]
    rows = OUT_ROWS
    n_groups = x_ref.shape[1] // rows
    x1s, h2s = [], []
    for r in range(n_groups):
        sl = pl.ds(r * rows, rows)
        rec = _rms(hs_ref[0, sl, :] * jax.nn.gelu(gr_ref[0, sl, :]), grec_ref[...]).astype(BF16)
        y = _dot(att_ref[0, sl, :], wo_ref[pl.ds(0, MLA_WIDTH), :])
        y = y + _dot(rec, wo_ref[pl.ds(MLA_WIDTH, LRU_WIDTH), :])
        y = y + _dot(sp_ref[0, sl, :], wo_ref[pl.ds(MLA_WIDTH + LRU_WIDTH, SGU_WIDTH), :])
        x1 = x_ref[0, sl, :] + mod[2:3] * y
        x1s.append(x1)
        h2s.append((_rms(x1, n2_ref[...]) * (1.0 + mod[4:5]) + mod[3:4]).astype(BF16))
    for r in range(n_groups):
        acc = None
        for c0, cw in FF_CHUNKS:
            g = _dot(h2s[r], wfi_ref[:, pl.ds(c0, cw)])
            u = _dot(h2s[r], wfi_ref[:, pl.ds(D_FF + c0, cw)])
            act = (jax.nn.silu(g) * u).astype(BF16)
            part = _dot(act, wfo_ref[pl.ds(c0, cw), :])
            acc = part if acc is None else acc + part
        x2 = x1s[r] + mod[5:6] * acc
        if final:
            x2 = _rms(x2, fn_ref[...])
        o_ref[0, pl.ds(r * rows, rows), :] = x2


def _out_call(x, att, hsum, gr, sp, mod, mod_row, lw, l, final_norm, final, tm):
    B, T, _ = x.shape
    tok = lambda w: pl.BlockSpec((1, tm, w), lambda b, t: (b, t, 0))
    W = functools.partial(_layer_spec, l=l)
    return pl.pallas_call(
        functools.partial(_out_kernel, final=final),
        out_shape=jax.ShapeDtypeStruct((B, T, D_MODEL), F32),
        grid=(B, T // tm),
        in_specs=[tok(D_MODEL), tok(MLA_WIDTH), tok(LRU_WIDTH), tok(LRU_WIDTH), tok(SGU_WIDTH),
                  _mod_spec(l, mod_row),
                  W((1, D_MODEL)),
                  W((1, LRU_WIDTH)),
                  W((D_MODEL, D_MODEL)),
                  W((D_MODEL, 2 * D_FF)),
                  W((D_FF, D_MODEL)),
                  _const_spec((1, D_MODEL))],
        out_specs=tok(D_MODEL),
        compiler_params=pltpu.CompilerParams(
            dimension_semantics=("parallel", "parallel"), vmem_limit_bytes=VMEM_LIMIT),
        name="out_ffn",
    )(x, att, hsum, gr, sp, mod, lw["norm2"], lw["g_rec"], lw["w_out"], lw["w_ffn_in"], lw["w_ffn_out"],
      final_norm)


def _rot_cols(w):
    q = ROPE // 4
    return jnp.concatenate([-w[..., q:2 * q], w[..., 0:q], -w[..., 3 * q:4 * q], w[..., 2 * q:3 * q]], axis=-1)


def _prep_params(p):
    L = DEPTH
    splits = np.cumsum([Q_LORA, KV_LORA, ROPE, LRU_WIDTH, LRU_WIDTH, SGU_WIDTH])
    qa, kva, kr, xr, gr, su, sv = jnp.split(p["w_in"].astype(BF16), splits, axis=2)
    kr_blk = jnp.concatenate([kr, _rot_cols(kr), jnp.zeros((L, D_MODEL, LANE - 2 * ROPE), BF16)], axis=2)
    w_in_ext = jnp.concatenate([qa, kva, kr_blk, xr, gr, su, sv], axis=2)

    wq = p["w_q_b"].astype(BF16).reshape(L, Q_LORA, HEADS, NOPE + ROPE)
    zpad = jnp.zeros((L, Q_LORA, HEADS, HEAD_PAD - NOPE - ROPE), BF16)
    w_q = jnp.concatenate([wq, zpad], axis=-1).reshape(L, Q_LORA, HEADS * HEAD_PAD)

    wkv = p["w_kv_b"].astype(BF16).reshape(L, KV_LORA, HEADS, NOPE + V_DIM)
    wk = jnp.concatenate([wkv[..., :NOPE], jnp.zeros((L, KV_LORA, HEADS, HEAD_PAD - NOPE), BF16)],
                         axis=-1).reshape(L, KV_LORA, HEADS * HEAD_PAD)
    wv = wkv[..., NOPE:].reshape(L, KV_LORA, MLA_WIDTH)
    place = np.zeros((LANE, HEADS, HEAD_PAD), np.float32)
    place[np.arange(ROPE), :, NOPE + np.arange(ROPE)] = 1.0
    place = jnp.broadcast_to(jnp.asarray(place.reshape(LANE, HEADS * HEAD_PAD), dtype=BF16),
                             (L, LANE, HEADS * HEAD_PAD))
    w_kv = jnp.concatenate([wk, place], axis=1)
    w_vt = jnp.swapaxes(wv, 1, 2)

    eye = np.eye(LRU_HEADS, dtype=np.float32)[None, :, None, :, None]

    def block_diag(w):
        _, hh, bi, bj = w.shape
        return (eye * w[:, :, :, None, :]).reshape(L, hh * bi, hh * bj)

    w_gates = (0.5 * jnp.concatenate([block_diag(p["lru_w_r"][:, 0]), block_diag(p["lru_w_i"][:, 0]),
                                      block_diag(p["lru_w_r"][:, 1]), block_diag(p["lru_w_i"][:, 1])],
                                     axis=2)).astype(BF16)
    b_gates = 0.5 * jnp.concatenate([p["lru_b_r"][:, 0], p["lru_b_i"][:, 0],
                                     p["lru_b_r"][:, 1], p["lru_b_i"][:, 1]], axis=1)[:, None]

    sgu_wcat = jnp.transpose(p["sgu_w"], (0, 2, 1, 3)).reshape(L, CHUNK, SGU_GROUPS * CHUNK).astype(BF16)
    sgu_bias = jnp.repeat(jnp.swapaxes(p["sgu_b"], 1, 2), SGU_GROUP_DIM, axis=2)
    gidx = np.arange(SGU_WIDTH) // SGU_GROUP_DIM
    gmat = jnp.asarray((gidx[:, None] == gidx[None, :]).astype(np.float32) / SGU_GROUP_DIM, dtype=BF16)

    on = p["out_norm"][:, None]
    row = lambda a: a[:, None]
    return {
        "norm1": row(p["norm1"]), "norm2": row(p["norm2"]),
        "w_in": w_in_ext, "q_a_norm": row(p["q_a_norm"]), "w_q": w_q,
        "kv_a_norm": row(p["kv_a_norm"]), "w_kv": w_kv, "w_vt": w_vt,
        "conv_w": p["conv_w"], "conv_b": row(p["conv_b"]),
        "w_gates": w_gates, "b_gates": b_gates, "lru_lam": p["lru_lam"],
        "sgu_wcat": sgu_wcat, "sgu_bias": sgu_bias, "sgu_norm": row(p["sgu_norm"]), "gmat": gmat,
        "g_att": on[..., :MLA_WIDTH], "g_rec": on[..., MLA_WIDTH:MLA_WIDTH + LRU_WIDTH],
        "g_sp": on[..., MLA_WIDTH + LRU_WIDTH:],
        "w_out": p["w_out"].astype(BF16), "w_ffn_in": p["w_ffn_in"].astype(BF16),
        "w_ffn_out": p["w_ffn_out"].astype(BF16),
    }


def _rope_tables(seq_len, ctx_len):
    f32 = np.float32
    rows = seq_len // GRID_W
    row = np.repeat(np.arange(rows, dtype=f32), GRID_W)
    col = np.tile(np.arange(GRID_W, dtype=f32), rows)
    half = ROPE // 2
    freq = (f32(ROPE_BASE) ** (-np.arange(0, half, 2, dtype=f32) / f32(half))).astype(f32)
    ar = row[:, None] * freq
    ac = col[:, None] * freq
    ang = np.concatenate([ar, ar, ac, ac], axis=-1).astype(f32)
    cos, sin = np.cos(ang).astype(f32), np.sin(ang).astype(f32)
    qs = f32(math.log2(math.e) / math.sqrt(NOPE + ROPE))

    up = ((np.arange(ROPE) // ROT_SHIFT) % 2 == 0).astype(f32)

    def q_tabs(c, s, n):
        pad = np.zeros((n, HEAD_PAD - NOPE - ROPE), f32)
        nope0 = np.zeros((n, NOPE), f32)
        cq = np.concatenate([np.ones((n, NOPE), f32), c, pad], axis=1)
        sa = np.concatenate([nope0, -s * up, pad], axis=1)
        sb = np.concatenate([nope0, s * (1 - up), pad], axis=1)
        tk = np.concatenate([c, s, np.zeros((n, LANE - 2 * ROPE), f32)], axis=1)
        return jnp.asarray(cq * qs), jnp.asarray(sa * qs), jnp.asarray(sb * qs), jnp.asarray(tk)

    lat = q_tabs(cos, sin, seq_len)
    ctx = q_tabs(np.ones((ctx_len, ROPE), f32), np.zeros((ctx_len, ROPE), f32), ctx_len)
    return lat, ctx


def kernel(x, c, ctx, c_ctx, norm1, norm2, w_ada, b_ada, w_in, q_a_norm, w_q_b, kv_a_norm, w_kv_b, conv_w, conv_b, lru_w_r, lru_b_r, lru_w_i, lru_b_i, lru_lam, sgu_norm, sgu_w, sgu_b, out_norm, w_out, w_ffn_in, w_ffn_out, final_norm):
    p = dict(norm1=norm1, norm2=norm2, w_in=w_in, q_a_norm=q_a_norm, w_q_b=w_q_b, kv_a_norm=kv_a_norm,
             w_kv_b=w_kv_b, conv_w=conv_w, conv_b=conv_b, lru_w_r=lru_w_r, lru_b_r=lru_b_r,
             lru_w_i=lru_w_i, lru_b_i=lru_b_i, lru_lam=lru_lam, sgu_norm=sgu_norm, sgu_w=sgu_w,
             sgu_b=sgu_b, out_norm=out_norm, w_out=w_out, w_ffn_in=w_ffn_in, w_ffn_out=w_ffn_out)
    B, S, _ = x.shape
    tctx = ctx.shape[1]
    mod_rows = ((B + 1 + SUBLANE - 1) // SUBLANE) * SUBLANE
    cc = jnp.concatenate([c, c_ctx[None], jnp.zeros((mod_rows - B - 1, D_MODEL), F32)], axis=0)
    mod = _ada_call(cc, w_ada, b_ada).reshape(DEPTH, mod_rows, 6, D_MODEL)
    tabs_l, tabs_c = _rope_tables(S, tctx)
    fn = final_norm[None]
    lw = _prep_params(p)

    h_ctx = ctx
    for l in range(DEPTH):
        last = l == DEPTH - 1
        q_l, k_l, v_l, xr_l, gr_l, sp_l = _inproj_call(x, mod, None, tabs_l, lw, l, 1024)
        q_c, k_c, v_c, xr_c, gr_c, sp_c = _inproj_call(h_ctx, mod, B, tabs_c, lw, l, 256)
        hs_c, hs_l = _lru_call(xr_c, xr_l, lw, l)
        att_l = _attn_call(q_l, [(k_c, v_c, 1), (k_l, v_l, 2)], lw["g_att"], l, 2048, 256)
        x = _out_call(x, att_l, hs_l, gr_l, sp_l, mod, None, lw, l, fn, last, 1024)
        if not last:
            att_c = _attn_call(q_c, [(k_c, v_c, 1)], lw["g_att"], l, 256, 256)
            h_ctx = _out_call(h_ctx, att_c, hs_c, gr_c, sp_c, mod, B, lw, l, fn, False, 256)
    return x
```
